```python
import jax, jax.numpy as jnp
from jax import lax
import numpy as np

D_MODEL = 4096
BATCH = 8
SEQ = 2048
DEPTH = 4

CHUNK = 64
MIX_WIDTH = D_MODEL
CONV_WIDTH = MIX_WIDTH // 2
CONV_K = 3
GLA_WIDTH = MIX_WIDTH - CONV_WIDTH
GLA_HEADS = 4
GLA_KEY_WIDTH = GLA_WIDTH // 2
GLA_DK = GLA_KEY_WIDTH // GLA_HEADS
GLA_DV = GLA_WIDTH // GLA_HEADS
GLA_GATE_RANK = 16
GLA_GATE_TAU = 16.0
D_FF = -(-8 * D_MODEL // (3 * 256)) * 256
NORM_EPS = 1e-6

IN_SPLITS = (CONV_WIDTH, CONV_WIDTH, CONV_WIDTH,
             GLA_KEY_WIDTH, GLA_KEY_WIDTH,
             GLA_WIDTH, GLA_WIDTH,
             GLA_GATE_RANK)
IN_COLS = sum(IN_SPLITS)

kernel_name = "hybrid_conv_gla_parallel_trunk"


def rmsnorm(x, g):
    xf = x.astype(jnp.float32)
    y = xf * lax.rsqrt(jnp.mean(xf * xf, axis=-1, keepdims=True) + NORM_EPS)
    return y.astype(x.dtype) * g


def short_conv_mixer(b, c, h, w_conv):
    u = c * h
    y = lax.conv_general_dilated(
        u, w_conv[:, None, :].astype(u.dtype), window_strides=(1,),
        padding=[(CONV_K - 1, 0)], dimension_numbers=("NWC", "WIO", "NWC"),
        feature_group_count=CONV_WIDTH)
    return b * y


def gla_mixer(q, k, v, g, a_low, w_a_up, b_a, norm_g):
    bsz, s, _ = q.shape
    n = s // CHUNK
    f32 = jnp.float32
    log_a = jax.nn.log_sigmoid((a_low @ w_a_up + b_a).astype(f32)) / GLA_GATE_TAU

    def heads(t, d):
        return t.astype(f32).reshape(bsz, n, CHUNK, GLA_HEADS, d).transpose(1, 0, 3, 2, 4)

    qh = heads(q, GLA_DK) * (GLA_DK ** -0.5)
    kh = heads(k, GLA_DK)
    vh = heads(v, GLA_DV)
    cum = jnp.cumsum(heads(log_a, GLA_DK), axis=3)
    total = cum[:, :, :, -1, :]
    q_dec = qh * jnp.exp(cum)
    k_inv = kh * jnp.exp(-cum)
    k_dec = kh * jnp.exp(total[:, :, :, None, :] - cum)

    mask = jnp.tril(jnp.ones((CHUNK, CHUNK), dtype=bool))
    scores = jnp.einsum("nbhid,nbhjd->nbhij", q_dec, k_inv)
    scores = jnp.where(mask, scores, 0.0)
    o_intra = jnp.einsum("nbhij,nbhjv->nbhiv", scores, vh)

    def step(state, xs):
        qd, kd, vc, tot = xs
        o = jnp.einsum("bhid,bhdv->bhiv", qd, state)
        state = jnp.exp(tot)[..., None] * state + jnp.einsum("bhjd,bhjv->bhdv", kd, vc)
        return state, o

    state0 = jnp.zeros((bsz, GLA_HEADS, GLA_DK, GLA_DV), f32)
    _, o_inter = lax.scan(step, state0, (q_dec, k_dec, vh, total))

    o = (o_intra + o_inter).transpose(1, 0, 3, 2, 4).reshape(bsz, s, GLA_HEADS, GLA_DV)
    o = o * lax.rsqrt(jnp.mean(o * o, axis=-1, keepdims=True) + NORM_EPS)
    o = o * norm_g.reshape(GLA_HEADS, GLA_DV).astype(f32)
    o = o.reshape(bsz, s, GLA_WIDTH) * jax.nn.silu(g.astype(f32))
    return o.astype(q.dtype)


def setup_inputs(seed: int = 0) -> dict:
    key = jax.random.key(seed)
    ks = jax.random.split(key, 16)
    f32 = jnp.float32

    def nrm(k, shape, scale):
        return jax.random.normal(k, shape, f32) * scale

    return {
        "x": nrm(ks[0], (BATCH, SEQ, D_MODEL), 1.0),
        "mix_norm": 1.0 + nrm(ks[1], (DEPTH, D_MODEL), 0.01),
        "w_in": nrm(ks[2], (DEPTH, D_MODEL, IN_COLS), D_MODEL ** -0.5),
        "conv_w": nrm(ks[3], (DEPTH, CONV_K, CONV_WIDTH), CONV_K ** -0.5),
        "gla_a_up": nrm(ks[4], (DEPTH, GLA_GATE_RANK, GLA_KEY_WIDTH), GLA_GATE_RANK ** -0.5),
        "gla_a_bias": nrm(ks[5], (DEPTH, GLA_KEY_WIDTH), 0.1),
        "gla_norm": 1.0 + nrm(ks[6], (DEPTH, GLA_WIDTH), 0.01),
        "w_out": nrm(ks[7], (DEPTH, MIX_WIDTH, D_MODEL), MIX_WIDTH ** -0.5),
        "ffn_norm": 1.0 + nrm(ks[8], (DEPTH, D_MODEL), 0.01),
        "w_gate": nrm(ks[9], (DEPTH, D_MODEL, D_FF), D_MODEL ** -0.5),
        "w_up": nrm(ks[10], (DEPTH, D_MODEL, D_FF), D_MODEL ** -0.5),
        "w_down": nrm(ks[11], (DEPTH, D_FF, D_MODEL), D_FF ** -0.5),
        "final_norm": 1.0 + nrm(ks[12], (D_MODEL,), 0.01),
    }


def reference(x, mix_norm, w_in, conv_w, gla_a_up, gla_a_bias, gla_norm, w_out,
              ffn_norm, w_gate, w_up, w_down, final_norm):
    offsets = np.cumsum(IN_SPLITS)[:-1].tolist()
    for l in range(DEPTH):
        hn = rmsnorm(x, mix_norm[l])
        proj = hn @ w_in[l]
        cb, cc, ch, q, k, v, g, a_low = jnp.split(proj, offsets, axis=-1)
        y_conv = short_conv_mixer(cb, cc, ch, conv_w[l])
        y_gla = gla_mixer(q, k, v, g, a_low, gla_a_up[l], gla_a_bias[l], gla_norm[l])
        x = x + jnp.concatenate([y_conv, y_gla], axis=-1) @ w_out[l]
        hn = rmsnorm(x, ffn_norm[l])
        x = x + (jax.nn.silu(hn @ w_gate[l]) * (hn @ w_up[l])) @ w_down[l]
    return rmsnorm(x, final_norm)
```

```python
import functools

import jax
import jax.numpy as jnp
from jax import lax
from jax.experimental import pallas as pl
from jax.experimental.pallas import tpu as pltpu

D_MODEL = 4096
CHUNK = 64
CONV_WIDTH = 2048
CONV_K = 3
GLA_WIDTH = 2048
GLA_HEADS = 4
GLA_DK = 256
GLA_DV = 512
GLA_GATE_RANK = 16
GLA_GATE_TAU = 16.0
NORM_EPS = 1e-6

OFF_CB, OFF_CC, OFF_CH = 0, CONV_WIDTH, 2 * CONV_WIDTH
OFF_Q = 3 * CONV_WIDTH
OFF_K = OFF_Q + GLA_HEADS * GLA_DK
OFF_V = OFF_K + GLA_HEADS * GLA_DK
OFF_G = OFF_V + GLA_WIDTH
OFF_A = OFF_G + GLA_WIDTH

LANES = 128
SUBLANES = 8
GLA_SUB = 256
VMEM_LIMIT = 56 * 1024 * 1024

_BF16 = jnp.bfloat16
_F32 = jnp.float32


def _params(*sem):
    return pltpu.CompilerParams(dimension_semantics=sem, vmem_limit_bytes=VMEM_LIMIT)


def _dot(a, b):
    return jnp.dot(a, b, preferred_element_type=_F32)


def _dot_nt(a, b):
    return lax.dot_general(a, b, (((1,), (1,)), ((), ())), preferred_element_type=_F32)


def _dot_tn(a, b):
    return lax.dot_general(a, b, (((0,), (0,)), ((), ())), preferred_element_type=_F32)


def _rmsnorm_kernel(x_ref, g_ref, o_ref):
    x = x_ref[...]
    ms = jnp.mean(x * x, axis=-1, keepdims=True)
    o_ref[...] = (x * lax.rsqrt(ms + NORM_EPS) * g_ref[...]).astype(o_ref.dtype)


def _rmsnorm(x, g, out_dtype, tm=256):
    m, d = x.shape
    return pl.pallas_call(
        _rmsnorm_kernel,
        grid=(m // tm,),
        in_specs=[pl.BlockSpec((tm, d), lambda i: (i, 0)),
                  pl.BlockSpec((1, d), lambda i: (0, 0))],
        out_specs=pl.BlockSpec((tm, d), lambda i: (i, 0)),
        out_shape=jax.ShapeDtypeStruct((m, d), out_dtype),
        compiler_params=_params("parallel"),
        name="rmsnorm",
    )(x, g.reshape(1, d))


def _conv_kernel(hn_ref, wb_ref, wc_ref, wh_ref, cw_ref, o_ref, carry_ref, *, tiles_per_seq):
    i = pl.program_id(1)

    @pl.when(i % tiles_per_seq == 0)
    def _():
        carry_ref[...] = jnp.zeros_like(carry_ref)

    hn = hn_ref[...]
    u = _dot(hn, wc_ref[...]) * _dot(hn, wh_ref[...])
    tm = u.shape[0]
    row = lax.broadcasted_iota(jnp.int32, u.shape, 0)
    prev = carry_ref[...]
    p1 = prev[SUBLANES - 1:SUBLANES, :]
    p2 = prev[SUBLANES - 2:SUBLANES - 1, :]
    u1 = jnp.where(row == 0, p1, pltpu.roll(u, 1, axis=0))
    u2 = jnp.where(row == 0, p2, jnp.where(row == 1, p1, pltpu.roll(u, 2, axis=0)))
    carry_ref[...] = u[tm - SUBLANES:, :]
    cw = cw_ref[...]
    y = cw[0:1, :] * u2 + cw[1:2, :] * u1 + cw[2:3, :] * u
    o_ref[...] = (_dot(hn, wb_ref[...]) * y).astype(o_ref.dtype)


def _conv_mixer(hn, w_in, conv_w, seq, tm=512, tn=512):
    m, d = hn.shape
    nb = CONV_WIDTH // tn
    kern = functools.partial(_conv_kernel, tiles_per_seq=seq // tm)
    wspec = lambda off: pl.BlockSpec((d, tn), lambda j, i, off=off: (0, off // tn + j))
    return pl.pallas_call(
        kern,
        grid=(nb, m // tm),
        in_specs=[pl.BlockSpec((tm, d), lambda j, i: (i, 0)),
                  wspec(OFF_CB), wspec(OFF_CC), wspec(OFF_CH),
                  pl.BlockSpec((CONV_K, tn), lambda j, i: (0, j))],
        out_specs=pl.BlockSpec((tm, tn), lambda j, i: (i, j)),
        out_shape=jax.ShapeDtypeStruct((m, CONV_WIDTH), _BF16),
        scratch_shapes=[pltpu.VMEM((SUBLANES, tn), _F32)],
        compiler_params=_params("parallel", "arbitrary"),
        name="conv_mixer",
    )(hn, w_in, w_in, w_in, conv_w)


def _split3(x):
    hi = x.astype(_BF16)
    r = x - hi.astype(_F32)
    mid = r.astype(_BF16)
    lo = (r - mid.astype(_F32)).astype(_BF16)
    return hi, mid, lo


def _gla_kernel(hn_ref, wq_ref, wk_ref, wv_ref, wg_ref, wa_ref, aup_ref, ab_ref, ng_ref,
                o_ref, st_ref):
    t = pl.program_id(2)

    @pl.when(t == 0)
    def _():
        st_ref[...] = jnp.zeros_like(st_ref)

    rows = hn_ref.shape[0]
    ri = lax.broadcasted_iota(jnp.int32, (GLA_SUB, GLA_SUB), 0)
    ci = lax.broadcasted_iota(jnp.int32, (GLA_SUB, GLA_SUB), 1)
    same_chunk = (ri // CHUNK) == (ci // CHUNK)
    causal = same_chunk & (ci <= ri)
    lower = causal.astype(_BF16)
    upper = (same_chunk & (ci > ri)).astype(_BF16)

    aup = aup_ref[...]
    aup_parts = _split3(aup)
    state = st_ref[...]

    for s in range(rows // GLA_SUB):
        hn = hn_ref[s * GLA_SUB:(s + 1) * GLA_SUB, :]
        q = _dot(hn, wq_ref[...]) * (GLA_DK ** -0.5)
        k = _dot(hn, wk_ref[...])
        v = _dot(hn, wv_ref[...]).astype(_BF16)
        a_low = _dot(hn, wa_ref[...])
        a_parts = _split3(a_low)
        z = ab_ref[...]
        for ia, ap in enumerate(a_parts):
            for iw, wp in enumerate(aup_parts):
                if ia + iw <= 2:
                    z = z + _dot(ap, wp)
        log_a = (jnp.minimum(z, 0.0) - jnp.log1p(jnp.exp(-jnp.abs(z)))) * (1.0 / GLA_GATE_TAU)
        la_parts = _split3(log_a)
        cum = _dot(lower, la_parts[0]) + _dot(lower, la_parts[1]) + _dot(lower, la_parts[2])
        rest = _dot(upper, la_parts[0]) + _dot(upper, la_parts[1]) + _dot(upper, la_parts[2])
        q_dec = (q * jnp.exp(cum)).astype(_BF16)
        k_inv = (k * jnp.exp(-cum)).astype(_BF16)
        k_dec = (k * jnp.exp(rest)).astype(_BF16)

        scores = jnp.where(causal, _dot_nt(q_dec, k_inv), 0.0).astype(_BF16)
        o = _dot(scores, v)

        inter = []
        for c in range(GLA_SUB // CHUNK):
            sl = slice(c * CHUNK, (c + 1) * CHUNK)
            inter.append(_dot_nt(q_dec[sl], state.astype(_BF16)))
            total = cum[(c + 1) * CHUNK - 1:(c + 1) * CHUNK, :]
            state = state * jnp.exp(total) + _dot_tn(v[sl], k_dec[sl])
        o = o + jnp.concatenate(inter, axis=0)

        o = o * lax.rsqrt(jnp.mean(o * o, axis=-1, keepdims=True) + NORM_EPS)
        g = _dot(hn, wg_ref[...])
        o = o * ng_ref[...] * (g / (1.0 + jnp.exp(-g)))
        o_ref[s * GLA_SUB:(s + 1) * GLA_SUB, :] = o.astype(o_ref.dtype)

    st_ref[...] = state


def _gla_mixer(hn, w_in, w_a, a_up, a_bias, norm_g, batch, seq, tr=512):
    m, d = hn.shape
    tps = seq // tr
    wspec = lambda off, width: pl.BlockSpec(
        (d, width), lambda h, b, t, off=off, width=width: (0, off // width + h))
    return pl.pallas_call(
        _gla_kernel,
        grid=(GLA_HEADS, batch, tps),
        in_specs=[pl.BlockSpec((tr, d), lambda h, b, t: (b * tps + t, 0)),
                  wspec(OFF_Q, GLA_DK), wspec(OFF_K, GLA_DK),
                  wspec(OFF_V, GLA_DV), wspec(OFF_G, GLA_DV),
                  pl.BlockSpec((d, LANES), lambda h, b, t: (0, 0)),
                  pl.BlockSpec((LANES, GLA_DK), lambda h, b, t: (0, h)),
                  pl.BlockSpec((1, GLA_DK), lambda h, b, t: (0, h)),
                  pl.BlockSpec((1, GLA_DV), lambda h, b, t: (0, h))],
        out_specs=pl.BlockSpec((tr, GLA_DV), lambda h, b, t: (b * tps + t, h)),
        out_shape=jax.ShapeDtypeStruct((m, GLA_WIDTH), _BF16),
        scratch_shapes=[pltpu.VMEM((GLA_DV, GLA_DK), _F32)],
        compiler_params=_params("parallel", "parallel", "arbitrary"),
        name="gla_mixer",
    )(hn, w_in, w_in, w_in, w_in, w_a, a_up, a_bias, norm_g)


def _out_kernel(yc_ref, yg_ref, w1_ref, w2_ref, x_ref, o_ref):
    o_ref[...] = x_ref[...] + _dot(yc_ref[...], w1_ref[...]) + _dot(yg_ref[...], w2_ref[...])


def _out_proj(y_conv, y_gla, w_out, x, tm=1024, tn=512):
    m, d = x.shape
    kc = y_conv.shape[1]
    kg = y_gla.shape[1]
    return pl.pallas_call(
        _out_kernel,
        grid=(m // tm, d // tn),
        in_specs=[pl.BlockSpec((tm, kc), lambda i, j: (i, 0)),
                  pl.BlockSpec((tm, kg), lambda i, j: (i, 0)),
                  pl.BlockSpec((kc, tn), lambda i, j: (0, j)),
                  pl.BlockSpec((kg, tn), lambda i, j: (1, j)),
                  pl.BlockSpec((tm, tn), lambda i, j: (i, j))],
        out_specs=pl.BlockSpec((tm, tn), lambda i, j: (i, j)),
        out_shape=jax.ShapeDtypeStruct((m, d), _F32),
        compiler_params=_params("parallel", "parallel"),
        name="out_proj",
    )(y_conv, y_gla, w_out, w_out, x)


def _gate_up_kernel(hn_ref, wg_ref, wu_ref, o_ref):
    hn = hn_ref[...]
    g = _dot(hn, wg_ref[...])
    u = _dot(hn, wu_ref[...])
    o_ref[...] = ((g / (1.0 + jnp.exp(-g))) * u).astype(o_ref.dtype)


def _gate_up(hn, w_gate, w_up, tm=2048, tf=256):
    m, d = hn.shape
    f = w_gate.shape[1]
    return pl.pallas_call(
        _gate_up_kernel,
        grid=(m // tm, f // tf),
        in_specs=[pl.BlockSpec((tm, d), lambda i, j: (i, 0)),
                  pl.BlockSpec((d, tf), lambda i, j: (0, j)),
                  pl.BlockSpec((d, tf), lambda i, j: (0, j))],
        out_specs=pl.BlockSpec((tm, tf), lambda i, j: (i, j)),
        out_shape=jax.ShapeDtypeStruct((m, f), _BF16),
        compiler_params=_params("parallel", "parallel"),
        name="ffn_gate_up",
    )(hn, w_gate, w_up)


def _down_kernel(h_ref, w_ref, x_ref, o_ref):
    o_ref[...] = x_ref[...] + _dot(h_ref[...], w_ref[...])


def _down_proj(h, w_down, x, tm=512, tn=512):
    m, d = x.shape
    f = h.shape[1]
    return pl.pallas_call(
        _down_kernel,
        grid=(m // tm, d // tn),
        in_specs=[pl.BlockSpec((tm, f), lambda i, j: (i, 0)),
                  pl.BlockSpec((f, tn), lambda i, j: (0, j)),
                  pl.BlockSpec((tm, tn), lambda i, j: (i, j))],
        out_specs=pl.BlockSpec((tm, tn), lambda i, j: (i, j)),
        out_shape=jax.ShapeDtypeStruct((m, d), _F32),
        compiler_params=_params("parallel", "parallel"),
        name="ffn_down",
    )(h, w_down, x)


def kernel(x, mix_norm, w_in, conv_w, gla_a_up, gla_a_bias, gla_norm, w_out,
           ffn_norm, w_gate, w_up, w_down, final_norm):
    batch, seq, d = x.shape
    depth = w_in.shape[0]
    xf = x.reshape(batch * seq, d)

    pad = LANES - GLA_GATE_RANK
    for l in range(depth):
        w_in_l = w_in[l].astype(_BF16)
        w_a = jnp.pad(w_in[l][:, OFF_A:], ((0, 0), (0, pad))).astype(_BF16)
        a_up = jnp.pad(gla_a_up[l], ((0, pad), (0, 0)))

        hn = _rmsnorm(xf, mix_norm[l], _BF16)
        y_conv = _conv_mixer(hn, w_in_l, conv_w[l], seq)
        y_gla = _gla_mixer(hn, w_in_l, w_a, a_up, gla_a_bias[l].reshape(1, -1),
                           gla_norm[l].reshape(1, -1), batch, seq)
        xf = _out_proj(y_conv, y_gla, w_out[l].astype(_BF16), xf)

        hn = _rmsnorm(xf, ffn_norm[l], _BF16)
        h = _gate_up(hn, w_gate[l].astype(_BF16), w_up[l].astype(_BF16))
        xf = _down_proj(h, w_down[l].astype(_BF16), xf)

    return _rmsnorm(xf, final_norm, x.dtype).reshape(batch, seq, d)
```

```python
import functools

import jax
import jax.numpy as jnp
from jax import lax
from jax.experimental import pallas as pl
from jax.experimental.pallas import tpu as pltpu

CHUNK = 64
CONV_WIDTH = 2048
CONV_K = 3
GLA_WIDTH = 2048
GLA_HEADS = 4
GLA_DK = 256
GLA_DV = 512
GLA_GATE_RANK = 16
GLA_GATE_TAU = 16.0
NORM_EPS = 1e-6

OFF_Q = 3 * CONV_WIDTH
OFF_K = OFF_Q + GLA_HEADS * GLA_DK
OFF_V = OFF_K + GLA_HEADS * GLA_DK
OFF_G = OFF_V + GLA_WIDTH
OFF_A = OFF_G + GLA_WIDTH

LANES = 128
SUBLANES = 8
GLA_SUB = 256
GLA_COLS = 2 * GLA_DK + 2 * GLA_DV + LANES
VMEM_LIMIT = 56 * 1024 * 1024

_BF16 = jnp.bfloat16
_F32 = jnp.float32


def _params(*sem):
    return pltpu.CompilerParams(dimension_semantics=sem, vmem_limit_bytes=VMEM_LIMIT)


def _dot(a, b):
    return jnp.dot(a, b, preferred_element_type=_F32)


def _dot_nt(a, b):
    return lax.dot_general(a, b, (((1,), (1,)), ((), ())), preferred_element_type=_F32)


def _dot_tn(a, b):
    return lax.dot_general(a, b, (((0,), (0,)), ((), ())), preferred_element_type=_F32)


def _lane_tile_sum(v):
    acc = v[:, 0:LANES]
    for c in range(1, v.shape[1] // LANES):
        acc = acc + v[:, c * LANES:(c + 1) * LANES]
    return acc


def _row_scale(ssq_ref, d):
    return lax.rsqrt(jnp.sum(ssq_ref[...], axis=-1, keepdims=True) * (1.0 / d) + NORM_EPS)


def _split2(x):
    hi = x.astype(_BF16)
    return hi, (x - hi.astype(_F32)).astype(_BF16)


def _prep_kernel(x_ref, g_ref, xg_ref, ssq_ref):
    x = x_ref[...]
    xg_ref[...] = (x * g_ref[...]).astype(xg_ref.dtype)
    ssq_ref[...] = _lane_tile_sum(x * x)


def _prep(x, g, tm=256):
    m, d = x.shape
    return pl.pallas_call(
        _prep_kernel,
        grid=(m // tm,),
        in_specs=[pl.BlockSpec((tm, d), lambda i: (i, 0)),
                  pl.BlockSpec((1, d), lambda i: (0, 0))],
        out_specs=[pl.BlockSpec((tm, d), lambda i: (i, 0)),
                   pl.BlockSpec((tm, LANES), lambda i: (i, 0))],
        out_shape=[jax.ShapeDtypeStruct((m, d), _BF16),
                   jax.ShapeDtypeStruct((m, LANES), _F32)],
        compiler_params=_params("parallel"),
        name="prep",
    )(x, g.reshape(1, d))


def _rmsnorm_kernel(x_ref, g_ref, o_ref):
    x = x_ref[...]
    ms = jnp.mean(x * x, axis=-1, keepdims=True)
    o_ref[...] = (x * lax.rsqrt(ms + NORM_EPS) * g_ref[...]).astype(o_ref.dtype)


def _rmsnorm(x, g, out_dtype, tm=256):
    m, d = x.shape
    return pl.pallas_call(
        _rmsnorm_kernel,
        grid=(m // tm,),
        in_specs=[pl.BlockSpec((tm, d), lambda i: (i, 0)),
                  pl.BlockSpec((1, d), lambda i: (0, 0))],
        out_specs=pl.BlockSpec((tm, d), lambda i: (i, 0)),
        out_shape=jax.ShapeDtypeStruct((m, d), out_dtype),
        compiler_params=_params("parallel"),
        name="rmsnorm",
    )(x, g.reshape(1, d))


def _conv_kernel(xg_ref, ssq_ref, w_ref, cw_ref, o_ref, carry_ref, *, tiles_per_seq):
    i = pl.program_id(1)

    @pl.when(i % tiles_per_seq == 0)
    def _():
        carry_ref[...] = jnp.zeros_like(carry_ref)

    tm, tn = o_ref.shape
    p = _dot(xg_ref[...], w_ref[...]) * _row_scale(ssq_ref, xg_ref.shape[1])
    u = p[:, tn:2 * tn] * p[:, 2 * tn:]
    row = lax.broadcasted_iota(jnp.int32, u.shape, 0)
    prev = carry_ref[...]
    p1 = prev[SUBLANES - 1:SUBLANES, :]
    p2 = prev[SUBLANES - 2:SUBLANES - 1, :]
    u1 = jnp.where(row == 0, p1, pltpu.roll(u, 1, axis=0))
    u2 = jnp.where(row == 0, p2, jnp.where(row == 1, p1, pltpu.roll(u, 2, axis=0)))
    carry_ref[...] = u[tm - SUBLANES:, :]
    cw = cw_ref[...]
    y = cw[0:1, :] * u2 + cw[1:2, :] * u1 + cw[2:3, :] * u
    o_ref[...] = (p[:, :tn] * y).astype(o_ref.dtype)


def _conv_mixer(xg, ssq, w_conv, conv_w, layer, seq, tm=512):
    m, d = xg.shape
    nb, tn = w_conv.shape[1], w_conv.shape[3] // 3
    kern = functools.partial(_conv_kernel, tiles_per_seq=seq // tm)
    return pl.pallas_call(
        kern,
        grid=(nb, m // tm),
        in_specs=[pl.BlockSpec((tm, d), lambda j, i: (i, 0)),
                  pl.BlockSpec((tm, LANES), lambda j, i: (i, 0)),
                  pl.BlockSpec((None, None, d, 3 * tn), lambda j, i: (layer, j, 0, 0)),
                  pl.BlockSpec((None, CONV_K, tn), lambda j, i: (layer, 0, j))],
        out_specs=pl.BlockSpec((tm, tn), lambda j, i: (i, j)),
        out_shape=jax.ShapeDtypeStruct((m, CONV_WIDTH), _BF16),
        scratch_shapes=[pltpu.VMEM((SUBLANES, tn), _F32)],
        compiler_params=_params("parallel", "arbitrary"),
        name="conv_mixer",
    )(xg, ssq, w_conv, conv_w)


def _gla_project(xg_ref, ssq_ref, w_ref, p_ref):
    p_ref[...] = _dot(xg_ref[...], w_ref[...]) * _row_scale(ssq_ref, xg_ref.shape[1])


def _gla_math(p_ref, aup_ref, ab_ref, ng_ref, o_ref, st_ref, first_tile):
    rows = p_ref.shape[0]
    ri = lax.broadcasted_iota(jnp.int32, (GLA_SUB, GLA_SUB), 0)
    ci = lax.broadcasted_iota(jnp.int32, (GLA_SUB, GLA_SUB), 1)
    causal = ((ri // CHUNK) == (ci // CHUNK)) & (ci <= ri)
    lower = causal.astype(_BF16)
    c_k, c_v, c_g, c_a = GLA_DK, 2 * GLA_DK, 2 * GLA_DK + GLA_DV, 2 * GLA_DK + 2 * GLA_DV

    a_hi, a_lo = _split2(p_ref[:, c_a:])
    w_hi, w_lo = _split2(aup_ref[...])
    z = ab_ref[...] + _dot(a_hi, w_hi) + _dot(a_lo, w_hi) + _dot(a_hi, w_lo)
    log_a = (jnp.minimum(z, 0.0) - jnp.log1p(jnp.exp(-jnp.abs(z)))) * (1.0 / GLA_GATE_TAU)
    la_hi, la_lo = _split2(log_a)

    state = jnp.where(first_tile, 0.0, st_ref[...])
    for s in range(rows // GLA_SUB):
        r = slice(s * GLA_SUB, (s + 1) * GLA_SUB)
        cum = _dot(lower, la_hi[r]) + _dot(lower, la_lo[r])
        totals = [cum[(c + 1) * CHUNK - 1:(c + 1) * CHUNK, :] for c in range(GLA_SUB // CHUNK)]
        total_b = jnp.concatenate([jnp.broadcast_to(tr, (CHUNK, GLA_DK)) for tr in totals], axis=0)
        q = p_ref[r, 0:c_k] * (GLA_DK ** -0.5)
        k = p_ref[r, c_k:c_v]
        q_dec = (q * jnp.exp(cum)).astype(_BF16)
        k_inv = (k * jnp.exp(-cum)).astype(_BF16)
        k_dec = (k * jnp.exp(total_b - cum)).astype(_BF16)
        vb = p_ref[r, c_v:c_g].astype(_BF16)

        scores = jnp.where(causal, _dot_nt(q_dec, k_inv), 0.0).astype(_BF16)
        o = _dot(scores, vb)

        inter = []
        for c in range(GLA_SUB // CHUNK):
            sl = slice(c * CHUNK, (c + 1) * CHUNK)
            inter.append(_dot_nt(q_dec[sl], state.astype(_BF16)))
            state = state * jnp.exp(totals[c]) + _dot_tn(vb[sl], k_dec[sl])
        o = o + jnp.concatenate(inter, axis=0)

        o = o * lax.rsqrt(jnp.mean(o * o, axis=-1, keepdims=True) + NORM_EPS)
        g = p_ref[r, c_g:c_a]
        o_ref[r, :] = (o * ng_ref[...] * (g / (1.0 + jnp.exp(-g)))).astype(o_ref.dtype)

    st_ref[...] = state


def _gla_kernel(xg_ref, ssq_ref, w_ref, aup_ref, ab_ref, ng_ref, o_ref, pa_ref, pb_ref, st_ref,
                *, tiles_per_seq):
    n = pl.program_id(1)

    @pl.when(n == 0)
    def _():
        pa_ref[...] = jnp.zeros_like(pa_ref)
        pb_ref[...] = jnp.zeros_like(pb_ref)
        st_ref[...] = jnp.zeros_like(st_ref)

    first_tile = (jnp.maximum(n - 1, 0) % tiles_per_seq) == 0

    @pl.when(n % 2 == 0)
    def _():
        _gla_project(xg_ref, ssq_ref, w_ref, pa_ref)
        _gla_math(pb_ref, aup_ref, ab_ref, ng_ref, o_ref, st_ref, first_tile)

    @pl.when(n % 2 == 1)
    def _():
        _gla_project(xg_ref, ssq_ref, w_ref, pb_ref)
        _gla_math(pa_ref, aup_ref, ab_ref, ng_ref, o_ref, st_ref, first_tile)


def _gla_mixer(xg, ssq, w_gla, a_up, a_bias, norm_g, layer, seq, tr=512):
    m, d = xg.shape
    nt = m // tr
    kern = functools.partial(_gla_kernel, tiles_per_seq=seq // tr)
    cur = lambda h, n: (jnp.minimum(n, nt - 1), 0)
    return pl.pallas_call(
        kern,
        grid=(GLA_HEADS, nt + 1),
        in_specs=[pl.BlockSpec((tr, d), cur),
                  pl.BlockSpec((tr, LANES), cur),
                  pl.BlockSpec((None, None, d, GLA_COLS), lambda h, n: (layer, h, 0, 0)),
                  pl.BlockSpec((None, LANES, GLA_DK), lambda h, n: (layer, 0, h)),
                  pl.BlockSpec((None, 1, GLA_DK), lambda h, n: (layer, 0, h)),
                  pl.BlockSpec((None, 1, GLA_DV), lambda h, n: (layer, 0, h))],
        out_specs=pl.BlockSpec((tr, GLA_DV), lambda h, n: (jnp.maximum(n - 1, 0), h)),
        out_shape=jax.ShapeDtypeStruct((m, GLA_WIDTH), _BF16),
        scratch_shapes=[pltpu.VMEM((tr, GLA_COLS), _F32),
                        pltpu.VMEM((tr, GLA_COLS), _F32),
                        pltpu.VMEM((GLA_DV, GLA_DK), _F32)],
        compiler_params=_params("parallel", "arbitrary"),
        name="gla_mixer",
    )(xg, ssq, w_gla, a_up, a_bias, norm_g)


def _init_ssq(ssq_ref):
    @pl.when(pl.program_id(1) == 0)
    def _():
        ssq_ref[...] = jnp.zeros_like(ssq_ref)


def _emit_stream(xn, gn_ref, o_ref, xg_ref, ssq_ref):
    o_ref[...] = xn
    xg_ref[...] = (xn * gn_ref[...]).astype(xg_ref.dtype)
    ssq_ref[...] += _lane_tile_sum(xn * xn)


def _out_kernel(yc_ref, yg_ref, w1_ref, w2_ref, x_ref, gn_ref, o_ref, xg_ref, ssq_ref):
    _init_ssq(ssq_ref)
    xn = x_ref[...] + _dot(yc_ref[...], w1_ref[...]) + _dot(yg_ref[...], w2_ref[...])
    _emit_stream(xn, gn_ref, o_ref, xg_ref, ssq_ref)


def _stream_out(m, d, tm, tn):
    specs = [pl.BlockSpec((tm, tn), lambda i, j: (i, j)),
             pl.BlockSpec((tm, tn), lambda i, j: (i, j)),
             pl.BlockSpec((tm, LANES), lambda i, j: (i, 0))]
    shapes = [jax.ShapeDtypeStruct((m, d), _F32),
              jax.ShapeDtypeStruct((m, d), _BF16),
              jax.ShapeDtypeStruct((m, LANES), _F32)]
    return specs, shapes


def _out_proj(y_conv, y_gla, w_out, x, g_next, layer, tm=1024, tn=512):
    m, d = x.shape
    kc = y_conv.shape[1]
    kg = y_gla.shape[1]
    out_specs, out_shape = _stream_out(m, d, tm, tn)
    return pl.pallas_call(
        _out_kernel,
        grid=(m // tm, d // tn),
        in_specs=[pl.BlockSpec((tm, kc), lambda i, j: (i, 0)),
                  pl.BlockSpec((tm, kg), lambda i, j: (i, 0)),
                  pl.BlockSpec((None, kc, tn), lambda i, j: (layer, 0, j)),
                  pl.BlockSpec((None, kg, tn), lambda i, j: (layer, 1, j)),
                  pl.BlockSpec((tm, tn), lambda i, j: (i, j)),
                  pl.BlockSpec((1, tn), lambda i, j: (0, j))],
        out_specs=out_specs,
        out_shape=out_shape,
        compiler_params=_params("parallel", "arbitrary"),
        name="out_proj",
    )(y_conv, y_gla, w_out, w_out, x, g_next.reshape(1, d))


def _down_kernel(h_ref, w_ref, x_ref, gn_ref, o_ref, xg_ref, ssq_ref):
    _init_ssq(ssq_ref)
    xn =x_ref[...] + _dot(h_ref[...], w_ref[...])
    _emit_stream(xn, gn_ref, o_ref, xg_ref, ssq_ref)


def _down_last_kernel(h_ref, w_ref, x_ref, o_ref):
    o_ref[...] = x_ref[...] + _dot(h_ref[...], w_ref[...])


def _down_proj(h, w_down, x, g_next, layer, tm=512, tn=512):
    m, d = x.shape
    f = h.shape[1]
    in_specs = [pl.BlockSpec((tm, f), lambda i, j: (i, 0)),
                pl.BlockSpec((None, f, tn), lambda i, j: (layer, 0, j)),
                pl.BlockSpec((tm, tn), lambda i, j: (i, j))]
    if g_next is None:
        return pl.pallas_call(
            _down_last_kernel,
            grid=(m // tm, d // tn),
            in_specs=in_specs,
            out_specs=pl.BlockSpec((tm, tn), lambda i, j: (i, j)),
            out_shape=jax.ShapeDtypeStruct((m, d), _F32),
            compiler_params=_params("parallel", "parallel"),
            name="ffn_down_last",
        )(h, w_down, x)
    out_specs, out_shape = _stream_out(m, d, tm, tn)
    return pl.pallas_call(
        _down_kernel,
        grid=(m // tm, d // tn),
        in_specs=in_specs + [pl.BlockSpec((1, tn), lambda i, j: (0, j))],
        out_specs=out_specs,
        out_shape=out_shape,
        compiler_params=_params("parallel", "arbitrary"),
        name="ffn_down",
    )(h, w_down, x, g_next.reshape(1, d))


def _gate_up_kernel(xg_ref, ssq_ref, wg_ref, wu_ref, o_ref, *, sub):
    wg = wg_ref[...].astype(_BF16)
    wu = wu_ref[...].astype(_BF16)
    rs_all = _row_scale(ssq_ref, xg_ref.shape[1])
    for r in range(xg_ref.shape[0] // sub):
        sl = slice(r * sub, (r + 1) * sub)
        xg = xg_ref[sl, :]
        rs = rs_all[sl]
        g = _dot(xg, wg) * rs
        u = _dot(xg, wu) * rs
        o_ref[sl, :] = ((g / (1.0 + jnp.exp(-g))) * u).astype(o_ref.dtype)


def _gate_up(xg, ssq, w_gate, w_up, layer, tm=2048, tf=256, sub=512):
    m, d = xg.shape
    f = w_gate.shape[2]
    return pl.pallas_call(
        functools.partial(_gate_up_kernel, sub=sub),
        grid=(m // tm, f // tf),
        in_specs=[pl.BlockSpec((tm, d), lambda i, j: (i, 0), pipeline_mode=pl.Buffered(1)),
                  pl.BlockSpec((tm, LANES), lambda i, j: (i, 0)),
                  pl.BlockSpec((None, d, tf), lambda i, j: (layer, 0, j)),
                  pl.BlockSpec((None, d, tf), lambda i, j: (layer, 0, j))],
        out_specs=pl.BlockSpec((tm, tf), lambda i, j: (i, j)),
        out_shape=jax.ShapeDtypeStruct((m, f), _BF16),
        compiler_params=_params("parallel", "parallel"),
        name="ffn_gate_up",
    )(xg, ssq, w_gate, w_up)


def _pack_weights(w_in, conv_tn=512):
    depth, d, _ = w_in.shape
    nb = CONV_WIDTH // conv_tn
    w_conv = w_in[:, :, :3 * CONV_WIDTH].reshape(depth, d, 3, nb, conv_tn)
    w_conv = w_conv.transpose(0, 3, 1, 2, 4).reshape(depth, nb, d, 3 * conv_tn).astype(_BF16)

    def heads(off, width):
        part = w_in[:, :, off:off + GLA_HEADS * width].reshape(depth, d, GLA_HEADS, width)
        return part.transpose(0, 2, 1, 3)
    w_a = jnp.pad(w_in[:, :, OFF_A:], ((0, 0), (0, 0), (0, LANES - GLA_GATE_RANK)))
    w_a = jnp.broadcast_to(w_a[:, None], (depth, GLA_HEADS, d, LANES))
    w_gla = jnp.concatenate([heads(OFF_Q, GLA_DK), heads(OFF_K, GLA_DK),
                             heads(OFF_V, GLA_DV), heads(OFF_G, GLA_DV), w_a], axis=-1).astype(_BF16)
    return w_conv, w_gla


def kernel(x, mix_norm, w_in, conv_w, gla_a_up, gla_a_bias, gla_norm, w_out,
           ffn_norm, w_gate, w_up, w_down, final_norm):
    batch, seq, d = x.shape
    depth = w_in.shape[0]
    xf = x.reshape(batch * seq, d)

    w_conv, w_gla = _pack_weights(w_in)
    w_out_b = w_out.astype(_BF16)
    w_down_b = w_down.astype(_BF16)
    a_up = jnp.pad(gla_a_up, ((0, 0), (0, LANES - GLA_GATE_RANK), (0, 0)))
    a_bias = gla_a_bias.reshape(depth, 1, -1)
    norm_g = gla_norm.reshape(depth, 1, -1)

    xg, ssq = _prep(xf, mix_norm[0])
    for l in range(depth):
        y_conv = _conv_mixer(xg, ssq, w_conv, conv_w, l, seq)
        y_gla = _gla_mixer(xg, ssq, w_gla, a_up, a_bias, norm_g, l, seq)
        xf, xg, ssq = _out_proj(y_conv, y_gla, w_out_b, xf, ffn_norm[l], l)
        h = _gate_up(xg, ssq, w_gate, w_up, l)
        if l + 1 < depth:
            xf, xg, ssq = _down_proj(h, w_down_b, xf, mix_norm[l + 1], l)
        else:
            xf = _down_proj(h, w_down_b, xf, None, l)

    return _rmsnorm(xf, final_norm, x.dtype).reshape(batch, seq, d)
```

```python
import functools

import jax
import jax.numpy as jnp
from jax import lax
from jax.experimental import pallas as pl
from jax.experimental.pallas import tpu as pltpu

CHUNK = 64
CONV_WIDTH = 2048
CONV_K = 3
GLA_WIDTH = 2048
GLA_HEADS = 4
GLA_DK = 256
GLA_DV = 512
GLA_GATE_RANK = 16
GLA_GATE_TAU = 16.0
NORM_EPS = 1e-6

OFF_Q = 3 * CONV_WIDTH
OFF_K = OFF_Q + GLA_HEADS * GLA_DK
OFF_V = OFF_K + GLA_HEADS * GLA_DK
OFF_G = OFF_V + GLA_WIDTH
OFF_A = OFF_G + GLA_WIDTH

LANES = 128
SUBLANES = 8
GLA_SUB = 256
GLA_COLS = 2 * GLA_DK + 2 * GLA_DV + LANES
VMEM_LIMIT = 56 * 1024 * 1024

_BF16 = jnp.bfloat16
_F32 = jnp.float32


def _params(*sem):
    return pltpu.CompilerParams(dimension_semantics=sem, vmem_limit_bytes=VMEM_LIMIT)


def _dot(a, b):
    return jnp.dot(a, b, preferred_element_type=_F32)


def _dot_nt(a, b):
    return lax.dot_general(a, b, (((1,), (1,)), ((), ())), preferred_element_type=_F32)


def _dot_tn(a, b):
    return lax.dot_general(a, b, (((0,), (0,)), ((), ())), preferred_element_type=_F32)


def _lane_tile_sum(v):
    acc = v[:, 0:LANES]
    for c in range(1, v.shape[1] // LANES):
        acc = acc + v[:, c * LANES:(c + 1) * LANES]
    return acc


def _row_scale(ssq_ref, d):
    return lax.rsqrt(jnp.sum(ssq_ref[...], axis=-1, keepdims=True) * (1.0 / d) + NORM_EPS)


def _split2(x):
    hi = x.astype(_BF16)
    return hi, (x - hi.astype(_F32)).astype(_BF16)


def _prep_kernel(x_ref, g_ref, xg_ref, ssq_ref):
    x = x_ref[...]
    xg_ref[...] = (x * g_ref[...]).astype(xg_ref.dtype)
    ssq_ref[...] = _lane_tile_sum(x * x)


def _prep(x, g, tm=256):
    m, d = x.shape
    return pl.pallas_call(
        _prep_kernel,
        grid=(m // tm,),
        in_specs=[pl.BlockSpec((tm, d), lambda i: (i, 0)),
                  pl.BlockSpec((1, d), lambda i: (0, 0))],
        out_specs=[pl.BlockSpec((tm, d), lambda i: (i, 0)),
                   pl.BlockSpec((tm, LANES), lambda i: (i, 0))],
        out_shape=[jax.ShapeDtypeStruct((m, d), _BF16),
                   jax.ShapeDtypeStruct((m, LANES), _F32)],
        compiler_params=_params("parallel"),
        name="prep",
    )(x, g.reshape(1, d))


def _rmsnorm_kernel(x_ref, g_ref, o_ref):
    x = x_ref[...]
    ms = jnp.mean(x * x, axis=-1, keepdims=True)
    o_ref[...] = (x * lax.rsqrt(ms + NORM_EPS) * g_ref[...]).astype(o_ref.dtype)


def _rmsnorm(x, g, out_dtype, tm=256):
    m, d = x.shape
    return pl.pallas_call(
        _rmsnorm_kernel,
        grid=(m // tm,),
        in_specs=[pl.BlockSpec((tm, d), lambda i: (i, 0)),
                  pl.BlockSpec((1, d), lambda i: (0, 0))],
        out_specs=pl.BlockSpec((tm, d), lambda i: (i, 0)),
        out_shape=jax.ShapeDtypeStruct((m, d), out_dtype),
        compiler_params=_params("parallel"),
        name="rmsnorm",
    )(x, g.reshape(1, d))


def _conv_kernel(xg_ref, ssq_ref, wb_ref, wc_ref, wh_ref, cw_ref, o_ref, carry_ref, *, tiles_per_seq):
    i = pl.program_id(1)

    @pl.when(i % tiles_per_seq == 0)
    def _():
        carry_ref[...] = jnp.zeros_like(carry_ref)

    tm, tn = o_ref.shape
    xg = xg_ref[...]
    rs = _row_scale(ssq_ref, xg_ref.shape[1])
    u = (_dot(xg, wc_ref[...]) * rs) * (_dot(xg, wh_ref[...]) * rs)
    row = lax.broadcasted_iota(jnp.int32, u.shape, 0)
    prev = carry_ref[...]
    p1 = prev[SUBLANES - 1:SUBLANES, :]
    p2 = prev[SUBLANES - 2:SUBLANES - 1, :]
    u1 = jnp.where(row == 0, p1, pltpu.roll(u, 1, axis=0))
    u2 = jnp.where(row == 0, p2, jnp.where(row == 1, p1, pltpu.roll(u, 2, axis=0)))
    carry_ref[...] = u[tm - SUBLANES:, :]
    cw = cw_ref[...]
    y = cw[0:1, :] * u2 + cw[1:2, :] * u1 + cw[2:3, :] * u
    o_ref[...] = (_dot(xg, wb_ref[...]) * rs * y).astype(o_ref.dtype)


def _conv_mixer(xg, ssq, w_in, conv_w, layer, seq, tm=1024, tn=512):
    m, d = xg.shape
    nb = CONV_WIDTH // tn
    kern = functools.partial(_conv_kernel, tiles_per_seq=seq // tm)
    wspec = lambda group: pl.BlockSpec((None, d, tn), lambda j, i: (layer, 0, group * nb + j))
    return pl.pallas_call(
        kern,
        grid=(nb, m // tm),
        in_specs=[pl.BlockSpec((tm, d), lambda j, i: (i, 0)),
                  pl.BlockSpec((tm, LANES), lambda j, i: (i, 0)),
                  wspec(0), wspec(1), wspec(2),
                  pl.BlockSpec((None, CONV_K, tn), lambda j, i: (layer, 0, j))],
        out_specs=pl.BlockSpec((tm, tn), lambda j, i: (i, j)),
        out_shape=jax.ShapeDtypeStruct((m, CONV_WIDTH), _BF16),
        scratch_shapes=[pltpu.VMEM((SUBLANES, tn), _F32)],
        compiler_params=_params("parallel", "arbitrary"),
        name="conv_mixer",
    )(xg, ssq, w_in, w_in, w_in, conv_w)


def _gla_project(xg_ref, ssq_ref, w_refs, p_ref):
    xg = xg_ref[...]
    rs = _row_scale(ssq_ref, xg_ref.shape[1])
    col = 0
    for w_ref in w_refs:
        width = w_ref.shape[1]
        p_ref[:, col:col + width] = _dot(xg, w_ref[...]) * rs
        col += width


def _gla_math(p_ref, aup_ref, ab_ref, ng_ref, o_ref, st_ref, first_tile):
    rows = p_ref.shape[0]
    nchunk = GLA_SUB // CHUNK
    ri = lax.broadcasted_iota(jnp.int32, (GLA_SUB, GLA_SUB), 0)
    ci = lax.broadcasted_iota(jnp.int32, (GLA_SUB, GLA_SUB), 1)
    lower = (((ri // CHUNK) == (ci // CHUNK)) & (ci <= ri)).astype(_BF16)
    c_k, c_v, c_g, c_a = GLA_DK, 2 * GLA_DK, 2 * GLA_DK + GLA_DV, 2 * GLA_DK + 2 * GLA_DV
    chunk = lambda c: slice(c * CHUNK, (c + 1) * CHUNK)

    a_hi, a_lo = _split2(p_ref[:, c_a:])
    w_hi, w_lo = _split2(aup_ref[...])
    z = ab_ref[...] + _dot(a_hi, w_hi) + _dot(a_lo, w_hi) + _dot(a_hi, w_lo)
    log_a = (jnp.minimum(z, 0.0) - jnp.log1p(jnp.exp(-jnp.abs(z)))) * (1.0 / GLA_GATE_TAU)
    la_hi, la_lo = _split2(log_a)

    state = jnp.where(first_tile, 0.0, st_ref[...])
    for s in range(rows // GLA_SUB):
        r = slice(s * GLA_SUB, (s + 1) * GLA_SUB)
        cum = _dot(lower, la_hi[r]) + _dot(lower, la_lo[r])
        totals = [cum[(c + 1) * CHUNK - 1:(c + 1) * CHUNK, :] for c in range(nchunk)]
        before = [jnp.zeros_like(totals[0])]
        for c in range(nchunk):
            before.append(before[c] + totals[c])
        total_b = jnp.concatenate([jnp.broadcast_to(tr, (CHUNK, GLA_DK)) for tr in totals], axis=0)
        k = p_ref[r, c_k:c_v]
        q_dec = p_ref[r, 0:c_k] * (GLA_DK ** -0.5) * jnp.exp(cum)
        k_inv = (k * jnp.exp(-cum)).astype(_BF16)
        k_dec = k * jnp.exp(total_b - cum)
        vb = p_ref[r, c_v:c_g].astype(_BF16)

        score_rows = []
        for c in range(nchunk):
            keys = [(k_dec[chunk(c2)] * jnp.exp(before[c] - before[c2 + 1])).astype(_BF16)
                    for c2 in range(c)]
            keys.append(k_inv[chunk(c)])
            if c + 1 < nchunk:
                keys.append(jnp.zeros(((nchunk - 1 - c) * CHUNK, GLA_DK), _BF16))
            score_rows.append(_dot_nt(q_dec[chunk(c)].astype(_BF16), jnp.concatenate(keys, axis=0)))
        scores = jnp.where(ci <= ri, jnp.concatenate(score_rows, axis=0), 0.0).astype(_BF16)

        q_state = jnp.concatenate([q_dec[chunk(c)] * jnp.exp(before[c]) for c in range(nchunk)], axis=0)
        o = _dot(scores, vb) + _dot_nt(q_state.astype(_BF16), state.astype(_BF16))
        k_end = jnp.concatenate([k_dec[chunk(c)] * jnp.exp(before[nchunk] - before[c + 1])
                                 for c in range(nchunk)], axis=0)
        state = state * jnp.exp(before[nchunk]) + _dot_tn(vb, k_end.astype(_BF16))

        o = o * lax.rsqrt(jnp.mean(o * o, axis=-1, keepdims=True) + NORM_EPS)
        g = p_ref[r, c_g:c_a]
        o_ref[r, :] = (o * ng_ref[...] * (g / (1.0 + jnp.exp(-g)))).astype(o_ref.dtype)

    st_ref[...] = state


def _gla_kernel(xg_ref, ssq_ref, wq_ref, wk_ref, wv_ref, wg_ref, wa_ref, aup_ref, ab_ref, ng_ref,
                o_ref, pa_ref, pb_ref, st_ref, *, tiles_per_seq):
    n = pl.program_id(1)
    w_refs = (wq_ref, wk_ref, wv_ref, wg_ref, wa_ref)

    @pl.when(n == 0)
    def _():
        pa_ref[...] = jnp.zeros_like(pa_ref)
        pb_ref[...] = jnp.zeros_like(pb_ref)
        st_ref[...] = jnp.zeros_like(st_ref)

    first_tile = (jnp.maximum(n - 1, 0) % tiles_per_seq) == 0

    @pl.when(n % 2 == 0)
    def _():
        _gla_project(xg_ref, ssq_ref, w_refs, pa_ref)
        _gla_math(pb_ref, aup_ref, ab_ref, ng_ref, o_ref, st_ref, first_tile)

    @pl.when(n % 2 == 1)
    def _():
        _gla_project(xg_ref, ssq_ref, w_refs, pb_ref)
        _gla_math(pa_ref, aup_ref, ab_ref, ng_ref, o_ref, st_ref, first_tile)


def _gla_mixer(xg, ssq, w_in, w_a, a_up, a_bias, norm_g, layer, seq, tr=512):
    m, d = xg.shape
    nt = m // tr
    kern = functools.partial(_gla_kernel, tiles_per_seq=seq // tr)
    cur = lambda h, n: (jnp.minimum(n, nt - 1), 0)
    wspec = lambda off, width: pl.BlockSpec((None, d, width), lambda h, n: (layer, 0, off // width + h))
    return pl.pallas_call(
        kern,
        grid=(GLA_HEADS, nt + 1),
        in_specs=[pl.BlockSpec((tr, d), cur),
                  pl.BlockSpec((tr, LANES), cur),
                  wspec(OFF_Q, GLA_DK), wspec(OFF_K, GLA_DK), wspec(OFF_V, GLA_DV), wspec(OFF_G, GLA_DV),
                  pl.BlockSpec((None, d, LANES), lambda h, n: (layer, 0, 0)),
                  pl.BlockSpec((None, LANES, GLA_DK), lambda h, n: (layer, 0, h)),
                  pl.BlockSpec((None, 1, GLA_DK), lambda h, n: (layer, 0, h)),
                  pl.BlockSpec((None, 1, GLA_DV), lambda h, n: (layer, 0, h))],
        out_specs=pl.BlockSpec((tr, GLA_DV), lambda h, n: (jnp.maximum(n - 1, 0), h)),
        out_shape=jax.ShapeDtypeStruct((m, GLA_WIDTH), _BF16),
        scratch_shapes=[pltpu.VMEM((tr, GLA_COLS), _F32),
                        pltpu.VMEM((tr, GLA_COLS), _F32),
                        pltpu.VMEM((GLA_DV, GLA_DK), _F32)],
        compiler_params=_params("parallel", "arbitrary"),
        name="gla_mixer",
    )(xg, ssq, w_in, w_in, w_in, w_in, w_a, a_up, a_bias, norm_g)


def _init_ssq(ssq_ref):
    @pl.when(pl.program_id(1) == 0)
    def _():
        ssq_ref[...] = jnp.zeros_like(ssq_ref)


def _emit_stream(xn, gn_ref, o_ref, xg_ref, ssq_ref):
    o_ref[...] = xn
    xg_ref[...] = (xn * gn_ref[...]).astype(xg_ref.dtype)
    ssq_ref[...] += _lane_tile_sum(xn * xn)


def _out_kernel(yc_ref, yg_ref, w1_ref, w2_ref, x_ref, gn_ref, o_ref, xg_ref, ssq_ref):
    _init_ssq(ssq_ref)
    xn = x_ref[...] + _dot(yc_ref[...], w1_ref[...]) + _dot(yg_ref[...], w2_ref[...])
    _emit_stream(xn, gn_ref, o_ref, xg_ref, ssq_ref)


def _stream_out(m, d, tm, tn):
    specs = [pl.BlockSpec((tm, tn), lambda i, j: (i, j)),
             pl.BlockSpec((tm, tn), lambda i, j: (i, j)),
             pl.BlockSpec((tm, LANES), lambda i, j: (i, 0))]
    shapes = [jax.ShapeDtypeStruct((m, d), _F32),
              jax.ShapeDtypeStruct((m, d), _BF16),
              jax.ShapeDtypeStruct((m, LANES), _F32)]
    return specs, shapes


def _out_proj(y_conv, y_gla, w_out, x, g_next, layer, tm=1024, tn=512):
    m, d = x.shape
    kc = y_conv.shape[1]
    kg = y_gla.shape[1]
    out_specs, out_shape = _stream_out(m, d, tm, tn)
    return pl.pallas_call(
        _out_kernel,
        grid=(m // tm, d // tn),
        in_specs=[pl.BlockSpec((tm, kc), lambda i, j: (i, 0)),
                  pl.BlockSpec((tm, kg), lambda i, j: (i, 0)),
                  pl.BlockSpec((None, kc, tn), lambda i, j: (layer, 0, j)),
                  pl.BlockSpec((None, kg, tn), lambda i, j: (layer, 1, j)),
                  pl.BlockSpec((tm, tn), lambda i, j: (i, j)),
                  pl.BlockSpec((1, tn), lambda i, j: (0, j))],
        out_specs=out_specs,
        out_shape=out_shape,
        compiler_params=_params("parallel", "arbitrary"),
        name="out_proj",
    )(y_conv, y_gla, w_out, w_out, x, g_next.reshape(1, d))


def _down_kernel(h_ref, w_ref, x_ref, gn_ref, o_ref, xg_ref, ssq_ref):
    _init_ssq(ssq_ref)
    xn = x_ref[...] + _dot(h_ref[...], w_ref[...])
    _emit_stream(xn, gn_ref, o_ref, xg_ref, ssq_ref)


def _down_last_kernel(h_ref, w_ref, x_ref, o_ref):
    o_ref[...] = x_ref[...] + _dot(h_ref[...], w_ref[...])


def _down_proj(h, w_down, x, g_next, layer, tm=512, tn=512):
    m, d = x.shape
    f = h.shape[1]
    in_specs = [pl.BlockSpec((tm, f), lambda i, j: (i, 0)),
                pl.BlockSpec((None, f, tn), lambda i, j: (layer, 0, j)),
                pl.BlockSpec((tm, tn), lambda i, j: (i, j))]
    if g_next is None:
        return pl.pallas_call(
            _down_last_kernel,
            grid=(m // tm, d // tn),
            in_specs=in_specs,
            out_specs=pl.BlockSpec((tm, tn), lambda i, j: (i, j)),
            out_shape=jax.ShapeDtypeStruct((m, d), _F32),
            compiler_params=_params("parallel", "parallel"),
            name="ffn_down_last",
        )(h, w_down, x)
    out_specs, out_shape = _stream_out(m, d, tm, tn)
    return pl.pallas_call(
        _down_kernel,
        grid=(m // tm, d // tn),
        in_specs=in_specs + [pl.BlockSpec((1, tn), lambda i, j: (0, j))],
        out_specs=out_specs,
        out_shape=out_shape,
        compiler_params=_params("parallel", "arbitrary"),
        name="ffn_down",
    )(h, w_down, x, g_next.reshape(1, d))


def _gate_up_kernel(xg_ref, ssq_ref, wg_ref, wu_ref, o_ref, *, sub):
    wg = wg_ref[...].astype(_BF16)
    wu = wu_ref[...].astype(_BF16)
    rs_all = _row_scale(ssq_ref, xg_ref.shape[1])
    for r in range(xg_ref.shape[0] // sub):
        sl = slice(r * sub, (r + 1) * sub)
        xg = xg_ref[sl, :]
        rs = rs_all[sl]
        g = _dot(xg, wg) * rs
        u = _dot(xg, wu) * rs
        o_ref[sl, :] = ((g / (1.0 + jnp.exp(-g))) * u).astype(o_ref.dtype)


def _gate_up(xg, ssq, w_gate, w_up, layer, tm=2048, tf=256, sub=512):
    m, d = xg.shape
    f = w_gate.shape[2]
    return pl.pallas_call(
        functools.partial(_gate_up_kernel, sub=sub),
        grid=(m // tm, f // tf),
        in_specs=[pl.BlockSpec((tm, d), lambda i, j: (i, 0), pipeline_mode=pl.Buffered(1)),
                  pl.BlockSpec((tm, LANES), lambda i, j: (i, 0)),
                  pl.BlockSpec((None, d, tf), lambda i, j: (layer, 0, j)),
                  pl.BlockSpec((None, d, tf), lambda i, j: (layer, 0, j))],
        out_specs=pl.BlockSpec((tm, tf), lambda i, j: (i, j)),
        out_shape=jax.ShapeDtypeStruct((m, f), _BF16),
        compiler_params=_params("parallel", "parallel"),
        name="ffn_gate_up",
    )(xg, ssq, w_gate, w_up)


def kernel(x, mix_norm, w_in, conv_w, gla_a_up, gla_a_bias, gla_norm, w_out,
           ffn_norm, w_gate, w_up, w_down, final_norm):
    batch, seq, d = x.shape
    depth = w_in.shape[0]
    xf = x.reshape(batch * seq, d)

    w_in_b = w_in.astype(_BF16)
    w_a = jnp.pad(w_in[:, :, OFF_A:], ((0, 0), (0, 0), (0, LANES - GLA_GATE_RANK))).astype(_BF16)
    w_out_b = w_out.astype(_BF16)
    w_down_b = w_down.astype(_BF16)
    a_up = jnp.pad(gla_a_up, ((0, 0), (0, LANES - GLA_GATE_RANK), (0, 0)))
    a_bias = gla_a_bias.reshape(depth, 1, -1)
    norm_g = gla_norm.reshape(depth, 1, -1)

    xg, ssq = _prep(xf, mix_norm[0])
    for l in range(depth):
        y_conv = _conv_mixer(xg, ssq, w_in_b, conv_w, l, seq)
        y_gla = _gla_mixer(xg, ssq, w_in_b, w_a, a_up, a_bias, norm_g, l, seq)
        xf, xg, ssq = _out_proj(y_conv, y_gla, w_out_b, xf, ffn_norm[l], l)
        h = _gate_up(xg, ssq, w_gate, w_up, l)
        if l + 1 < depth:
            xf, xg, ssq = _down_proj(h, w_down_b, xf, mix_norm[l + 1], l)
        else:
            xf = _down_proj(h, w_down_b, xf, None, l)

    return _rmsnorm(xf, final_norm, x.dtype).reshape(batch, seq, d)
```

```python
import functools

import jax
import jax.numpy as jnp
from jax import lax
from jax.experimental import pallas as pl
from jax.experimental.pallas import tpu as pltpu

CHUNK = 64
CONV_WIDTH = 2048
CONV_K = 3
GLA_WIDTH = 2048
GLA_HEADS = 4
GLA_DK = 256
GLA_DV = 512
GLA_GATE_RANK = 16
GLA_GATE_TAU = 16.0
NORM_EPS = 1e-6

OFF_Q = 3 * CONV_WIDTH
OFF_K = OFF_Q + GLA_HEADS * GLA_DK
OFF_V = OFF_K + GLA_HEADS * GLA_DK
OFF_G = OFF_V + GLA_WIDTH
OFF_A = OFF_G + GLA_WIDTH

LANES = 128
SUBLANES = 8
GLA_SUB = 256
GLA_COLS = 2 * GLA_DK + 2 * GLA_DV
VMEM_LIMIT = 56 * 1024 * 1024
VMEM_LIMIT_GATE_UP = 60 * 1024 * 1024

_BF16 = jnp.bfloat16
_F32 = jnp.float32


def _params(*sem, vmem=VMEM_LIMIT):
    return pltpu.CompilerParams(dimension_semantics=sem, vmem_limit_bytes=vmem)


def _dot(a, b):
    return jnp.dot(a, b, preferred_element_type=_F32)


def _dot_nt(a, b):
    return lax.dot_general(a, b, (((1,), (1,)), ((), ())), preferred_element_type=_F32)


def _dot_tn(a, b):
    return lax.dot_general(a, b, (((0,), (0,)), ((), ())), preferred_element_type=_F32)


def _lane_tile_sum(v):
    acc = v[:, 0:LANES]
    for c in range(1, v.shape[1] // LANES):
        acc = acc + v[:, c * LANES:(c + 1) * LANES]
    return acc


def _row_scale(ssq_ref, d):
    return lax.rsqrt(jnp.sum(ssq_ref[...], axis=-1, keepdims=True) * (1.0 / d) + NORM_EPS)


def _split2(x):
    hi = x.astype(_BF16)
    return hi, (x - hi.astype(_F32)).astype(_BF16)


def _prep_kernel(x_ref, g_ref, xg_ref, ssq_ref):
    x = x_ref[...]
    xg_ref[...] = (x * g_ref[...]).astype(xg_ref.dtype)
    ssq_ref[...] = _lane_tile_sum(x * x)


def _prep(x, g, tm=256):
    m, d = x.shape
    return pl.pallas_call(
        _prep_kernel,
        grid=(m // tm,),
        in_specs=[pl.BlockSpec((tm, d), lambda i: (i, 0)),
                  pl.BlockSpec((1, d), lambda i: (0, 0))],
        out_specs=[pl.BlockSpec((tm, d), lambda i: (i, 0)),
                   pl.BlockSpec((tm, LANES), lambda i: (i, 0))],
        out_shape=[jax.ShapeDtypeStruct((m, d), _BF16),
                   jax.ShapeDtypeStruct((m, LANES), _F32)],
        compiler_params=_params("parallel"),
        name="prep",
    )(x, g.reshape(1, d))


def _rmsnorm_kernel(x_ref, g_ref, o_ref):
    x = x_ref[...]
    ms = jnp.mean(x * x, axis=-1, keepdims=True)
    o_ref[...] = (x * lax.rsqrt(ms + NORM_EPS) * g_ref[...]).astype(o_ref.dtype)


def _rmsnorm(x, g, out_dtype, tm=256):
    m, d = x.shape
    return pl.pallas_call(
        _rmsnorm_kernel,
        grid=(m // tm,),
        in_specs=[pl.BlockSpec((tm, d), lambda i: (i, 0)),
                  pl.BlockSpec((1, d), lambda i: (0, 0))],
        out_specs=pl.BlockSpec((tm, d), lambda i: (i, 0)),
        out_shape=jax.ShapeDtypeStruct((m, d), out_dtype),
        compiler_params=_params("parallel"),
        name="rmsnorm",
    )(x, g.reshape(1, d))


def _conv_kernel(xg_ref, ssq_ref, wb_ref, wc_ref, wh_ref, cw_ref, o_ref, carry_ref, *, tiles_per_seq):
    i = pl.program_id(1)

    @pl.when(i % tiles_per_seq == 0)
    def _():
        carry_ref[...] = jnp.zeros_like(carry_ref)

    tm, tn = o_ref.shape
    xg = xg_ref[...]
    rs = _row_scale(ssq_ref, xg_ref.shape[1])
    u = (_dot(xg, wc_ref[...]) * rs) * (_dot(xg, wh_ref[...]) * rs)
    row = lax.broadcasted_iota(jnp.int32, u.shape, 0)
    prev = carry_ref[...]
    p1 = prev[SUBLANES - 1:SUBLANES, :]
    p2 = prev[SUBLANES - 2:SUBLANES - 1, :]
    u1 = jnp.where(row == 0, p1, pltpu.roll(u, 1, axis=0))
    u2 = jnp.where(row == 0, p2, jnp.where(row == 1, p1, pltpu.roll(u, 2, axis=0)))
    carry_ref[...] = u[tm - SUBLANES:, :]
    cw = cw_ref[...]
    y = cw[0:1, :] * u2 + cw[1:2, :] * u1 + cw[2:3, :] * u
    o_ref[...] = (_dot(xg, wb_ref[...]) * rs * y).astype(o_ref.dtype)


def _conv_mixer(xg, ssq, w_in, conv_w, layer, seq, tm=1024, tn=512):
    m, d = xg.shape
    nb = CONV_WIDTH // tn
    kern = functools.partial(_conv_kernel, tiles_per_seq=seq // tm)
    wspec = lambda group: pl.BlockSpec((None, d, tn), lambda j, i: (layer, 0, group * nb + j))
    return pl.pallas_call(
        kern,
        grid=(nb, m // tm),
        in_specs=[pl.BlockSpec((tm, d), lambda j, i: (i, 0)),
                  pl.BlockSpec((tm, LANES), lambda j, i: (i, 0)),
                  wspec(0), wspec(1), wspec(2),
                  pl.BlockSpec((None, CONV_K, tn), lambda j, i: (layer, 0, j))],
        out_specs=pl.BlockSpec((tm, tn), lambda j, i: (i, j)),
        out_shape=jax.ShapeDtypeStruct((m, CONV_WIDTH), _BF16),
        scratch_shapes=[pltpu.VMEM((SUBLANES, tn), _F32)],
        compiler_params=_params("parallel", "arbitrary"),
        name="conv_mixer",
    )(xg, ssq, w_in, w_in, w_in, conv_w)


def _gate_low_kernel(xg_ref, ssq_ref, wa_ref, o_ref):
    o_ref[...] = _dot(xg_ref[...], wa_ref[...]) * _row_scale(ssq_ref, xg_ref.shape[1])


def _gate_low(xg, ssq, w_a, layer, tm=1024):
    m, d = xg.shape
    return pl.pallas_call(
        _gate_low_kernel,
        grid=(m // tm,),
        in_specs=[pl.BlockSpec((tm, d), lambda i: (i, 0)),
                  pl.BlockSpec((tm, LANES), lambda i: (i, 0)),
                  pl.BlockSpec((None, d, LANES), lambda i: (layer, 0, 0))],
        out_specs=pl.BlockSpec((tm, LANES), lambda i: (i, 0)),
        out_shape=jax.ShapeDtypeStruct((m, LANES), _F32),
        compiler_params=_params("parallel"),
        name="gate_low",
    )(xg, ssq, w_a)


def _gla_project_pieces(xg_ref, ssq_ref, w_refs, p_ref):
    d = xg_ref.shape[1]
    pieces = []
    for half in range(xg_ref.shape[0] // GLA_SUB):
        r = slice(half * GLA_SUB, (half + 1) * GLA_SUB)
        col = 0
        for w_ref in w_refs:
            width = w_ref.shape[1]

            def piece(r=r, col=col, width=width, w_ref=w_ref):
                rs = _row_scale(ssq_ref[r, :], d)
                p_ref[r, col:col + width] = _dot(xg_ref[r, :], w_ref[...]) * rs
            pieces.append(piece)
            col += width
    return pieces


def _gla_math(p_ref, a_ref, aup_ref, ab_ref, ng_ref, o_ref, st_ref, first_tile):
    rows = p_ref.shape[0]
    nchunk = GLA_SUB // CHUNK
    ri = lax.broadcasted_iota(jnp.int32, (GLA_SUB, GLA_SUB), 0)
    ci = lax.broadcasted_iota(jnp.int32, (GLA_SUB, GLA_SUB), 1)
    lower = (((ri // CHUNK) == (ci // CHUNK)) & (ci <= ri)).astype(_BF16)
    c_k, c_v, c_g = GLA_DK, 2 * GLA_DK, 2 * GLA_DK + GLA_DV
    chunk = lambda c: slice(c * CHUNK, (c + 1) * CHUNK)

    a_hi, a_lo = _split2(a_ref[...])
    w_hi, w_lo = _split2(aup_ref[...])
    z = ab_ref[...] + _dot(a_hi, w_hi) + _dot(a_lo, w_hi) + _dot(a_hi, w_lo)
    log_a = (jnp.minimum(z, 0.0) - jnp.log1p(jnp.exp(-jnp.abs(z)))) * (1.0 / GLA_GATE_TAU)
    la_hi, la_lo = _split2(log_a)
    yield

    state = jnp.where(first_tile, 0.0, st_ref[...])
    for s in range(rows // GLA_SUB):
        r = slice(s * GLA_SUB, (s + 1) * GLA_SUB)
        cum = _dot(lower, la_hi[r]) + _dot(lower, la_lo[r])
        totals = [cum[(c + 1) * CHUNK - 1:(c + 1) * CHUNK, :] for c in range(nchunk)]
        before = [jnp.zeros_like(totals[0])]
        for c in range(nchunk):
            before.append(before[c] + totals[c])
        total_b = jnp.concatenate([jnp.broadcast_to(tr, (CHUNK, GLA_DK)) for tr in totals], axis=0)
        k = p_ref[r, c_k:c_v]
        q_dec = p_ref[r, 0:c_k] * (GLA_DK ** -0.5) * jnp.exp(cum)
        k_inv = (k * jnp.exp(-cum)).astype(_BF16)
        k_dec = k * jnp.exp(total_b - cum)
        vb = p_ref[r, c_v:c_g].astype(_BF16)
        yield

        score_rows = []
        for c in range(nchunk):
            keys = [(k_dec[chunk(c2)] * jnp.exp(before[c] - before[c2 + 1])).astype(_BF16)
                    for c2 in range(c)]
            keys.append(k_inv[chunk(c)])
            if c + 1 < nchunk:
                keys.append(jnp.zeros(((nchunk - 1 - c) * CHUNK, GLA_DK), _BF16))
            score_rows.append(_dot_nt(q_dec[chunk(c)].astype(_BF16), jnp.concatenate(keys, axis=0)))
        scores = jnp.where(ci <= ri, jnp.concatenate(score_rows, axis=0), 0.0).astype(_BF16)
        yield

        q_state = jnp.concatenate([q_dec[chunk(c)] * jnp.exp(before[c]) for c in range(nchunk)], axis=0)
        o = _dot(scores, vb) + _dot_nt(q_state.astype(_BF16), state.astype(_BF16))
        k_end = jnp.concatenate([k_dec[chunk(c)] * jnp.exp(before[nchunk] - before[c + 1])
                                 for c in range(nchunk)], axis=0)
        yield
        state = state * jnp.exp(before[nchunk]) + _dot_tn(vb, k_end.astype(_BF16))
        yield

        o = o * lax.rsqrt(jnp.mean(o * o, axis=-1, keepdims=True) + NORM_EPS)
        g = p_ref[r, c_g:]
        o_ref[r, :] = (o * ng_ref[...] * (g / (1.0 + jnp.exp(-g)))).astype(o_ref.dtype)
        yield

    st_ref[...] = state


def _gla_kernel(xg_ref, ssq_ref, a_ref, wq_ref, wk_ref, wv_ref, wg_ref, aup_ref, ab_ref, ng_ref,
                o_ref, pa_ref, pb_ref, st_ref, *, tiles_per_seq):
    n = pl.program_id(1)
    w_refs = (wq_ref, wk_ref, wv_ref, wg_ref)

    @pl.when(n == 0)
    def _():
        pa_ref[...] = jnp.zeros_like(pa_ref)
        pb_ref[...] = jnp.zeros_like(pb_ref)
        st_ref[...] = jnp.zeros_like(st_ref)

    first_tile = (jnp.maximum(n - 1, 0) % tiles_per_seq) == 0

    def step(p_new, p_old):
        pieces = _gla_project_pieces(xg_ref, ssq_ref, w_refs, p_new)
        for _ in _gla_math(p_old, a_ref, aup_ref, ab_ref, ng_ref, o_ref, st_ref, first_tile):
            if pieces:
                pieces.pop(0)()
        for piece in pieces:
            piece()

    @pl.when(n % 2 == 0)
    def _():
        step(pa_ref, pb_ref)

    @pl.when(n % 2 == 1)
    def _():
        step(pb_ref, pa_ref)


def _gla_mixer(xg, ssq, a_low, w_in, a_up, a_bias, norm_g, layer, seq, tr=512):
    m, d = xg.shape
    nt = m // tr
    kern = functools.partial(_gla_kernel, tiles_per_seq=seq // tr)
    cur = lambda h, n: (jnp.minimum(n, nt - 1), 0)
    prev = lambda h, n: (jnp.maximum(n - 1, 0), 0)
    wspec = lambda off, width: pl.BlockSpec((None, d, width), lambda h, n: (layer, 0, off // width + h))
    return pl.pallas_call(
        kern,
        grid=(GLA_HEADS, nt + 1),
        in_specs=[pl.BlockSpec((tr, d), cur),
                  pl.BlockSpec((tr, LANES), cur),
                  pl.BlockSpec((tr, LANES), prev),
                  wspec(OFF_Q, GLA_DK), wspec(OFF_K, GLA_DK), wspec(OFF_V, GLA_DV), wspec(OFF_G, GLA_DV),
                  pl.BlockSpec((None, LANES, GLA_DK), lambda h, n: (layer, 0, h)),
                  pl.BlockSpec((None, 1, GLA_DK), lambda h, n: (layer, 0, h)),
                  pl.BlockSpec((None, 1, GLA_DV), lambda h, n: (layer, 0, h))],
        out_specs=pl.BlockSpec((tr, GLA_DV), lambda h, n: (jnp.maximum(n - 1, 0), h)),
        out_shape=jax.ShapeDtypeStruct((m, GLA_WIDTH), _BF16),
        scratch_shapes=[pltpu.VMEM((tr, GLA_COLS), _F32),
                        pltpu.VMEM((tr, GLA_COLS), _F32),
                        pltpu.VMEM((GLA_DV, GLA_DK), _F32)],
        compiler_params=_params("parallel", "arbitrary"),
        name="gla_mixer",
    )(xg, ssq, a_low, w_in, w_in, w_in, w_in, a_up, a_bias, norm_g)


def _init_ssq(ssq_ref):
    @pl.when(pl.program_id(1) == 0)
    def _():
        ssq_ref[...] = jnp.zeros_like(ssq_ref)


def _emit_stream(xn, gn_ref, o_ref, xg_ref, ssq_ref):
    o_ref[...] = xn
    xg_ref[...] = (xn * gn_ref[...]).astype(xg_ref.dtype)
    ssq_ref[...] += _lane_tile_sum(xn * xn)


def _out_kernel(yc_ref, yg_ref, w1_ref, w2_ref, x_ref, gn_ref, o_ref, xg_ref, ssq_ref):
    _init_ssq(ssq_ref)
    xn = x_ref[...] + _dot(yc_ref[...], w1_ref[...]) + _dot(yg_ref[...], w2_ref[...])
    _emit_stream(xn, gn_ref, o_ref, xg_ref, ssq_ref)


def _stream_out(m, d, tm, tn):
    specs = [pl.BlockSpec((tm, tn), lambda i, j: (i, j)),
             pl.BlockSpec((tm, tn), lambda i, j: (i, j)),
             pl.BlockSpec((tm, LANES), lambda i, j: (i, 0))]
    shapes = [jax.ShapeDtypeStruct((m, d), _F32),
              jax.ShapeDtypeStruct((m, d), _BF16),
              jax.ShapeDtypeStruct((m, LANES), _F32)]
    return specs, shapes


def _out_proj(y_conv, y_gla, w_out, x, g_next, layer, tm=1024, tn=512):
    m, d = x.shape
    kc = y_conv.shape[1]
    kg = y_gla.shape[1]
    out_specs, out_shape = _stream_out(m, d, tm, tn)
    return pl.pallas_call(
        _out_kernel,
        grid=(m // tm, d // tn),
        in_specs=[pl.BlockSpec((tm, kc), lambda i, j: (i, 0)),
                  pl.BlockSpec((tm, kg), lambda i, j: (i, 0)),
                  pl.BlockSpec((None, kc, tn), lambda i, j: (layer, 0, j)),
                  pl.BlockSpec((None, kg, tn), lambda i, j: (layer, 1, j)),
                  pl.BlockSpec((tm, tn), lambda i, j: (i, j)),
                  pl.BlockSpec((1, tn), lambda i, j: (0, j))],
        out_specs=out_specs,
        out_shape=out_shape,
        compiler_params=_params("parallel", "arbitrary"),
        name="out_proj",
    )(y_conv, y_gla, w_out, w_out, x, g_next.reshape(1, d))


def _down_kernel(h_ref, w_ref, x_ref, gn_ref, o_ref, xg_ref, ssq_ref):
    _init_ssq(ssq_ref)
    xn = x_ref[...] + _dot(h_ref[...], w_ref[...])
    _emit_stream(xn, gn_ref, o_ref, xg_ref, ssq_ref)


def _down_last_kernel(h_ref, w_ref, x_ref, o_ref):
    o_ref[...] = x_ref[...] + _dot(h_ref[...], w_ref[...])


def _down_proj(h, w_down, x, g_next, layer, tm=512, tn=512):
    m, d = x.shape
    f = h.shape[1]
    in_specs = [pl.BlockSpec((tm, f), lambda i, j: (i, 0)),
                pl.BlockSpec((None, f, tn), lambda i, j: (layer, 0, j)),
                pl.BlockSpec((tm, tn), lambda i, j: (i, j))]
    if g_next is None:
        return pl.pallas_call(
            _down_last_kernel,
            grid=(m // tm, d // tn),
            in_specs=in_specs,
            out_specs=pl.BlockSpec((tm, tn), lambda i, j: (i, j)),
            out_shape=jax.ShapeDtypeStruct((m, d), _F32),
            compiler_params=_params("parallel", "parallel"),
            name="ffn_down_last",
        )(h, w_down, x)
    out_specs, out_shape = _stream_out(m, d, tm, tn)
    return pl.pallas_call(
        _down_kernel,
        grid=(m // tm, d // tn),
        in_specs=in_specs + [pl.BlockSpec((1, tn), lambda i, j: (0, j))],
        out_specs=out_specs,
        out_shape=out_shape,
        compiler_params=_params("parallel", "arbitrary"),
        name="ffn_down",
    )(h, w_down, x, g_next.reshape(1, d))


def _gate_up_kernel(xg_ref, ssq_ref, wg_ref, wu_ref, o_ref, *, sub):
    wg = wg_ref[...].astype(_BF16)
    wu = wu_ref[...].astype(_BF16)
    rs_all = _row_scale(ssq_ref, xg_ref.shape[1])
    for r in range(xg_ref.shape[0] // sub):
        sl = slice(r * sub, (r + 1) * sub)
        xg = xg_ref[sl, :]
        rs = rs_all[sl]
        g = _dot(xg, wg) * rs
        u = _dot(xg, wu) * rs
        o_ref[sl, :] = ((g / (1.0 + jnp.exp(-g))) * u).astype(o_ref.dtype)


def _gate_up(xg, ssq, w_gate, w_up, layer, tm=2048, tf=256, sub=512):
    m, d = xg.shape
    f = w_gate.shape[2]
    return pl.pallas_call(
        functools.partial(_gate_up_kernel, sub=sub),
        grid=(m // tm, f // tf),
        in_specs=[pl.BlockSpec((tm, d), lambda i, j: (i, 0)),
                  pl.BlockSpec((tm, LANES), lambda i, j: (i, 0)),
                  pl.BlockSpec((None, d, tf), lambda i, j: (layer, 0, j)),
                  pl.BlockSpec((None, d, tf), lambda i, j: (layer, 0, j))],
        out_specs=pl.BlockSpec((tm, tf), lambda i, j: (i, j)),
        out_shape=jax.ShapeDtypeStruct((m, f), _BF16),
        compiler_params=_params("parallel", "parallel", vmem=VMEM_LIMIT_GATE_UP),
        name="ffn_gate_up",
    )(xg, ssq, w_gate, w_up)


def kernel(x, mix_norm, w_in, conv_w, gla_a_up, gla_a_bias, gla_norm, w_out,
           ffn_norm, w_gate, w_up, w_down, final_norm):
    batch, seq, d = x.shape
    depth = w_in.shape[0]
    xf = x.reshape(batch * seq, d)

    w_in_b = w_in[:, :, :OFF_A].astype(_BF16)
    w_a = jnp.pad(w_in[:, :, OFF_A:], ((0, 0), (0, 0), (0, LANES - GLA_GATE_RANK))).astype(_BF16)
    w_out_b = w_out.astype(_BF16)
    w_down_b = w_down.astype(_BF16)
    a_up = jnp.pad(gla_a_up, ((0, 0), (0, LANES - GLA_GATE_RANK), (0, 0)))
    a_bias = gla_a_bias.reshape(depth, 1, -1)
    norm_g = gla_norm.reshape(depth, 1, -1)

    xg, ssq = _prep(xf, mix_norm[0])
    for l in range(depth):
        y_conv = _conv_mixer(xg, ssq, w_in_b, conv_w, l, seq)
        a_low = _gate_low(xg, ssq, w_a, l)
        y_gla = _gla_mixer(xg, ssq, a_low, w_in_b, a_up, a_bias, norm_g, l, seq)
        xf, xg, ssq = _out_proj(y_conv, y_gla, w_out_b, xf, ffn_norm[l], l)
        h = _gate_up(xg, ssq, w_gate, w_up, l)
        if l + 1 < depth:
            xf, xg, ssq = _down_proj(h, w_down_b, xf, mix_norm[l + 1], l)
        else:
            xf = _down_proj(h, w_down_b, xf, None, l)

    return _rmsnorm(xf, final_norm, x.dtype).reshape(batch, seq, d)
```

```python
import functools

import jax
import jax.numpy as jnp
from jax import lax
from jax.experimental import pallas as pl
from jax.experimental.pallas import tpu as pltpu

CHUNK = 64
CONV_WIDTH = 2048
CONV_K = 3
GLA_WIDTH = 2048
GLA_HEADS = 4
GLA_DK = 256
GLA_DV = 512
GLA_GATE_RANK = 16
GLA_GATE_TAU = 16.0
NORM_EPS = 1e-6

OFF_Q = 3 * CONV_WIDTH
OFF_K = OFF_Q + GLA_HEADS * GLA_DK
OFF_V = OFF_K + GLA_HEADS * GLA_DK
OFF_G = OFF_V + GLA_WIDTH
OFF_A = OFF_G + GLA_WIDTH

LANES = 128
SUBLANES = 8
GLA_SUB = 256
GLA_COLS = 2 * GLA_DK + 2 * GLA_DV
VMEM_LIMIT = 56 * 1024 * 1024
VMEM_LIMIT_GATE_UP = 60 * 1024 * 1024

_BF16 = jnp.bfloat16
_F32 = jnp.float32


def _params(*sem, vmem=VMEM_LIMIT):
    return pltpu.CompilerParams(dimension_semantics=sem, vmem_limit_bytes=vmem)


def _dot(a, b):
    return jnp.dot(a, b, preferred_element_type=_F32)


def _dot_nt(a, b):
    return lax.dot_general(a, b, (((1,), (1,)), ((), ())), preferred_element_type=_F32)


def _dot_tn(a, b):
    return lax.dot_general(a, b, (((0,), (0,)), ((), ())), preferred_element_type=_F32)


def _lane_tile_sum(v):
    acc = v[:, 0:LANES]
    for c in range(1, v.shape[1] // LANES):
        acc = acc + v[:, c * LANES:(c + 1) * LANES]
    return acc


def _row_scale(ssq_ref, d):
    return lax.rsqrt(jnp.sum(ssq_ref[...], axis=-1, keepdims=True) * (1.0 / d) + NORM_EPS)


def _split2(x):
    hi = x.astype(_BF16)
    return hi, (x - hi.astype(_F32)).astype(_BF16)


def _cast_kernel(w_ref, o_ref):
    o_ref[...] = w_ref[:, :o_ref.shape[1]].astype(o_ref.dtype)


def _cast_w_in(w_in, cols, tr=256):
    depth, d, n = w_in.shape
    return pl.pallas_call(
        _cast_kernel,
        grid=(depth, d // tr),
        in_specs=[pl.BlockSpec((None, tr, n), lambda l, i: (l, i, 0))],
        out_specs=pl.BlockSpec((None, tr, cols), lambda l, i: (l, i, 0)),
        out_shape=jax.ShapeDtypeStruct((depth, d, cols), _BF16),
        compiler_params=_params("parallel", "parallel"),
        name="cast_w_in",
    )(w_in)


def _prep_kernel(x_ref, g_ref, xg_ref, ssq_ref):
    x = x_ref[...]
    xg_ref[...] = (x * g_ref[...]).astype(xg_ref.dtype)
    ssq_ref[...] = _lane_tile_sum(x * x)


def _prep(x, g, tm=256):
    m, d = x.shape
    return pl.pallas_call(
        _prep_kernel,
        grid=(m // tm,),
        in_specs=[pl.BlockSpec((tm, d), lambda i: (i, 0)),
                  pl.BlockSpec((1, d), lambda i: (0, 0))],
        out_specs=[pl.BlockSpec((tm, d), lambda i: (i, 0)),
                   pl.BlockSpec((tm, LANES), lambda i: (i, 0))],
        out_shape=[jax.ShapeDtypeStruct((m, d), _BF16),
                   jax.ShapeDtypeStruct((m, LANES), _F32)],
        compiler_params=_params("parallel"),
        name="prep",
    )(x, g.reshape(1, d))


def _rmsnorm_kernel(x_ref, g_ref, o_ref):
    x = x_ref[...]
    ms = jnp.mean(x * x, axis=-1, keepdims=True)
    o_ref[...] = (x * lax.rsqrt(ms + NORM_EPS) * g_ref[...]).astype(o_ref.dtype)


def _rmsnorm(x, g, out_dtype, tm=256):
    m, d = x.shape
    return pl.pallas_call(
        _rmsnorm_kernel,
        grid=(m // tm,),
        in_specs=[pl.BlockSpec((tm, d), lambda i: (i, 0)),
                  pl.BlockSpec((1, d), lambda i: (0, 0))],
        out_specs=pl.BlockSpec((tm, d), lambda i: (i, 0)),
        out_shape=jax.ShapeDtypeStruct((m, d), out_dtype),
        compiler_params=_params("parallel"),
        name="rmsnorm",
    )(x, g.reshape(1, d))


def _conv_kernel(xg_ref, ssq_ref, wb_ref, wc_ref, wh_ref, cw_ref, o_ref, carry_ref, *, tiles_per_seq):
    i = pl.program_id(1)

    @pl.when(i % tiles_per_seq == 0)
    def _():
        carry_ref[...] = jnp.zeros_like(carry_ref)

    tm, tn = o_ref.shape
    xg = xg_ref[...]
    rs = _row_scale(ssq_ref, xg_ref.shape[1])
    u = (_dot(xg, wc_ref[...]) * rs) * (_dot(xg, wh_ref[...]) * rs)
    row = lax.broadcasted_iota(jnp.int32, u.shape, 0)
    prev = carry_ref[...]
    p1 = prev[SUBLANES - 1:SUBLANES, :]
    p2 = prev[SUBLANES - 2:SUBLANES - 1, :]
    u1 = jnp.where(row == 0, p1, pltpu.roll(u, 1, axis=0))
    u2 = jnp.where(row == 0, p2, jnp.where(row == 1, p1, pltpu.roll(u, 2, axis=0)))
    carry_ref[...] = u[tm - SUBLANES:, :]
    cw = cw_ref[...]
    y = cw[0:1, :] * u2 + cw[1:2, :] * u1 + cw[2:3, :] * u
    o_ref[...] = (_dot(xg, wb_ref[...]) * rs * y).astype(o_ref.dtype)


def _conv_mixer(xg, ssq, w_in, conv_w, layer, seq, tm=1024, tn=512):
    m, d = xg.shape
    nb = CONV_WIDTH // tn
    kern = functools.partial(_conv_kernel, tiles_per_seq=seq // tm)
    wspec = lambda group: pl.BlockSpec((None, d, tn), lambda j, i: (layer, 0, group * nb + j))
    return pl.pallas_call(
        kern,
        grid=(nb, m // tm),
        in_specs=[pl.BlockSpec((tm, d), lambda j, i: (i, 0)),
                  pl.BlockSpec((tm, LANES), lambda j, i: (i, 0)),
                  wspec(0), wspec(1), wspec(2),
                  pl.BlockSpec((None, CONV_K, tn), lambda j, i: (layer, 0, j))],
        out_specs=pl.BlockSpec((tm, tn), lambda j, i: (i, j)),
        out_shape=jax.ShapeDtypeStruct((m, CONV_WIDTH), _BF16),
        scratch_shapes=[pltpu.VMEM((SUBLANES, tn), _F32)],
        compiler_params=_params("parallel", "arbitrary"),
        name="conv_mixer",
    )(xg, ssq, w_in, w_in, w_in, conv_w)


def _gate_low_kernel(xg_ref, ssq_ref, wa_ref, o_ref):
    o_ref[...] = _dot(xg_ref[...], wa_ref[...]) * _row_scale(ssq_ref, xg_ref.shape[1])


def _gate_low(xg, ssq, w_a, layer, tm=1024):
    m, d = xg.shape
    return pl.pallas_call(
        _gate_low_kernel,
        grid=(m // tm,),
        in_specs=[pl.BlockSpec((tm, d), lambda i: (i, 0)),
                  pl.BlockSpec((tm, LANES), lambda i: (i, 0)),
                  pl.BlockSpec((None, d, LANES), lambda i: (layer, 0, 0))],
        out_specs=pl.BlockSpec((tm, LANES), lambda i: (i, 0)),
        out_shape=jax.ShapeDtypeStruct((m, LANES), _F32),
        compiler_params=_params("parallel"),
        name="gate_low",
    )(xg, ssq, w_a)


def _gla_project_pieces(xg_ref, ssq_ref, w_refs, p_ref):
    d = xg_ref.shape[1]
    pieces = []
    for half in range(xg_ref.shape[0] // GLA_SUB):
        r = slice(half * GLA_SUB, (half + 1) * GLA_SUB)
        col = 0
        for w_ref in w_refs:
            width = w_ref.shape[1]

            def piece(r=r, col=col, width=width, w_ref=w_ref):
                rs = _row_scale(ssq_ref[r, :], d)
                p_ref[r, col:col + width] = _dot(xg_ref[r, :], w_ref[...]) * rs
            pieces.append(piece)
            col += width
    return pieces


def _gla_math(p_ref, a_ref, aup_ref, ab_ref, ng_ref, o_ref, st_ref, first_tile):
    rows = p_ref.shape[0]
    nchunk = GLA_SUB // CHUNK
    ri = lax.broadcasted_iota(jnp.int32, (GLA_SUB, GLA_SUB), 0)
    ci = lax.broadcasted_iota(jnp.int32, (GLA_SUB, GLA_SUB), 1)
    lower = (((ri // CHUNK) == (ci // CHUNK)) & (ci <= ri)).astype(_BF16)
    c_k, c_v, c_g = GLA_DK, 2 * GLA_DK, 2 * GLA_DK + GLA_DV
    chunk = lambda c: slice(c * CHUNK, (c + 1) * CHUNK)

    a_hi, a_lo = _split2(a_ref[...])
    w_hi, w_lo = _split2(aup_ref[...])
    z = ab_ref[...] + _dot(a_hi, w_hi) + _dot(a_lo, w_hi) + _dot(a_hi, w_lo)
    log_a = (jnp.minimum(z, 0.0) - jnp.log1p(jnp.exp(-jnp.abs(z)))) * (1.0 / GLA_GATE_TAU)
    la_hi, la_lo = _split2(log_a)
    yield

    state = jnp.where(first_tile, 0.0, st_ref[...])
    for s in range(rows // GLA_SUB):
        r = slice(s * GLA_SUB, (s + 1) * GLA_SUB)
        cum = _dot(lower, la_hi[r]) + _dot(lower, la_lo[r])
        totals = [cum[(c + 1) * CHUNK - 1:(c + 1) * CHUNK, :] for c in range(nchunk)]
        before = [jnp.zeros_like(totals[0])]
        for c in range(nchunk):
            before.append(before[c] + totals[c])
        total_b = jnp.concatenate([jnp.broadcast_to(tr, (CHUNK, GLA_DK)) for tr in totals], axis=0)
        k = p_ref[r, c_k:c_v]
        q_dec = p_ref[r, 0:c_k] * (GLA_DK ** -0.5) * jnp.exp(cum)
        k_inv = (k * jnp.exp(-cum)).astype(_BF16)
        k_dec = k * jnp.exp(total_b - cum)
        vb = p_ref[r, c_v:c_g].astype(_BF16)
        yield

        score_rows = []
        for c in range(nchunk):
            keys = [(k_dec[chunk(c2)] * jnp.exp(before[c] - before[c2 + 1])).astype(_BF16)
                    for c2 in range(c)]
            keys.append(k_inv[chunk(c)])
            if c + 1 < nchunk:
                keys.append(jnp.zeros(((nchunk - 1 - c) * CHUNK, GLA_DK), _BF16))
            score_rows.append(_dot_nt(q_dec[chunk(c)].astype(_BF16), jnp.concatenate(keys, axis=0)))
        scores = jnp.where(ci <= ri, jnp.concatenate(score_rows, axis=0), 0.0).astype(_BF16)
        yield

        q_state = jnp.concatenate([q_dec[chunk(c)] * jnp.exp(before[c]) for c in range(nchunk)], axis=0)
        o = _dot(scores, vb) + _dot_nt(q_state.astype(_BF16), state.astype(_BF16))
        k_end = jnp.concatenate([k_dec[chunk(c)] * jnp.exp(before[nchunk] - before[c + 1])
                                 for c in range(nchunk)], axis=0)
        yield
        state = state * jnp.exp(before[nchunk]) + _dot_tn(vb, k_end.astype(_BF16))
        yield

        o = o * lax.rsqrt(jnp.mean(o * o, axis=-1, keepdims=True) + NORM_EPS)
        g = p_ref[r, c_g:]
        o_ref[r, :] = (o * ng_ref[...] * (g / (1.0 + jnp.exp(-g)))).astype(o_ref.dtype)
        yield

    st_ref[...] = state


def _gla_kernel(xg_ref, ssq_ref, a_ref, wq_ref, wk_ref, wv_ref, wg_ref, aup_ref, ab_ref, ng_ref,
                o_ref, pa_ref, pb_ref, st_ref, *, tiles_per_seq):
    n = pl.program_id(1)
    w_refs = (wq_ref, wk_ref, wv_ref, wg_ref)

    @pl.when(n == 0)
    def _():
        pa_ref[...] = jnp.zeros_like(pa_ref)
        pb_ref[...] = jnp.zeros_like(pb_ref)
        st_ref[...] = jnp.zeros_like(st_ref)

    first_tile = (jnp.maximum(n - 1, 0) % tiles_per_seq) == 0

    def step(p_new, p_old):
        pieces = _gla_project_pieces(xg_ref, ssq_ref, w_refs, p_new)
        for _ in _gla_math(p_old, a_ref, aup_ref, ab_ref, ng_ref, o_ref, st_ref, first_tile):
            if pieces:
                pieces.pop(0)()
        for piece in pieces:
            piece()

    @pl.when(n % 2 == 0)
    def _():
        step(pa_ref, pb_ref)

    @pl.when(n % 2 == 1)
    def _():
        step(pb_ref, pa_ref)


def _gla_mixer(xg, ssq, a_low, w_in, a_up, a_bias, norm_g, layer, seq, tr=512):
    m, d = xg.shape
    nt = m // tr
    kern = functools.partial(_gla_kernel, tiles_per_seq=seq // tr)
    cur = lambda h, n: (jnp.minimum(n, nt - 1), 0)
    prev = lambda h, n: (jnp.maximum(n - 1, 0), 0)
    wspec = lambda off, width: pl.BlockSpec((None, d, width), lambda h, n: (layer, 0, off // width + h))
    return pl.pallas_call(
        kern,
        grid=(GLA_HEADS, nt + 1),
        in_specs=[pl.BlockSpec((tr, d), cur),
                  pl.BlockSpec((tr, LANES), cur),
                  pl.BlockSpec((tr, LANES), prev),
                  wspec(OFF_Q, GLA_DK), wspec(OFF_K, GLA_DK), wspec(OFF_V, GLA_DV), wspec(OFF_G, GLA_DV),
                  pl.BlockSpec((None, LANES, GLA_DK), lambda h, n: (layer, 0, h)),
                  pl.BlockSpec((None, 1, GLA_DK), lambda h, n: (layer, 0, h)),
                  pl.BlockSpec((None, 1, GLA_DV), lambda h, n: (layer, 0, h))],
        out_specs=pl.BlockSpec((tr, GLA_DV), lambda h, n: (jnp.maximum(n - 1, 0), h)),
        out_shape=jax.ShapeDtypeStruct((m, GLA_WIDTH), _BF16),
        scratch_shapes=[pltpu.VMEM((tr, GLA_COLS), _F32),
                        pltpu.VMEM((tr, GLA_COLS), _F32),
                        pltpu.VMEM((GLA_DV, GLA_DK), _F32)],
        compiler_params=_params("parallel", "arbitrary"),
        name="gla_mixer",
    )(xg, ssq, a_low, w_in, w_in, w_in, w_in, a_up, a_bias, norm_g)


def _init_ssq(ssq_ref):
    @pl.when(pl.program_id(1) == 0)
    def _():
        ssq_ref[...] = jnp.zeros_like(ssq_ref)


def _emit_stream(xn, gn_ref, o_ref, xg_ref, ssq_ref):
    o_ref[...] = xn
    xg_ref[...] = (xn * gn_ref[...]).astype(xg_ref.dtype)
    ssq_ref[...] += _lane_tile_sum(xn * xn)


def _out_kernel(yc_ref, yg_ref, w1_ref, w2_ref, x_ref, gn_ref, o_ref, xg_ref, ssq_ref):
    _init_ssq(ssq_ref)
    xn = x_ref[...] + _dot(yc_ref[...], w1_ref[...]) + _dot(yg_ref[...], w2_ref[...])
    _emit_stream(xn, gn_ref, o_ref, xg_ref, ssq_ref)


def _stream_out(m, d, tm, tn):
    specs = [pl.BlockSpec((tm, tn), lambda i, j: (i, j)),
             pl.BlockSpec((tm, tn), lambda i, j: (i, j)),
             pl.BlockSpec((tm, LANES), lambda i, j: (i, 0))]
    shapes = [jax.ShapeDtypeStruct((m, d), _F32),
              jax.ShapeDtypeStruct((m, d), _BF16),
              jax.ShapeDtypeStruct((m, LANES), _F32)]
    return specs, shapes


def _out_proj(y_conv, y_gla, w_out, x, g_next, layer, tm=1024, tn=512):
    m, d = x.shape
    kc = y_conv.shape[1]
    kg = y_gla.shape[1]
    out_specs, out_shape = _stream_out(m, d, tm, tn)
    return pl.pallas_call(
        _out_kernel,
        grid=(m // tm, d // tn),
        in_specs=[pl.BlockSpec((tm, kc), lambda i, j: (i, 0)),
                  pl.BlockSpec((tm, kg), lambda i, j: (i, 0)),
                  pl.BlockSpec((None, kc, tn), lambda i, j: (layer, 0, j)),
                  pl.BlockSpec((None, kg, tn), lambda i, j: (layer, 1, j)),
                  pl.BlockSpec((tm, tn), lambda i, j: (i, j)),
                  pl.BlockSpec((1, tn), lambda i, j: (0, j))],
        out_specs=out_specs,
        out_shape=out_shape,
        compiler_params=_params("parallel", "arbitrary"),
        name="out_proj",
    )(y_conv, y_gla, w_out, w_out, x, g_next.reshape(1, d))


def _down_kernel(h_ref, w_ref, x_ref, gn_ref, o_ref, xg_ref, ssq_ref):
    _init_ssq(ssq_ref)
    xn = x_ref[...] + _dot(h_ref[...], w_ref[...])
    _emit_stream(xn, gn_ref, o_ref, xg_ref, ssq_ref)


def _down_last_kernel(h_ref, w_ref, x_ref, o_ref):
    o_ref[...] = x_ref[...] + _dot(h_ref[...], w_ref[...])


def _down_proj(h, w_down, x, g_next, layer, tm=512, tn=512):
    m, d = x.shape
    f = h.shape[1]
    in_specs = [pl.BlockSpec((tm, f), lambda i, j: (i, 0)),
                pl.BlockSpec((None, f, tn), lambda i, j: (layer, 0, j)),
                pl.BlockSpec((tm, tn), lambda i, j: (i, j))]
    if g_next is None:
        return pl.pallas_call(
            _down_last_kernel,
            grid=(m // tm, d // tn),
            in_specs=in_specs,
            out_specs=pl.BlockSpec((tm, tn), lambda i, j: (i, j)),
            out_shape=jax.ShapeDtypeStruct((m, d), _F32),
            compiler_params=_params("parallel", "parallel"),
            name="ffn_down_last",
        )(h, w_down, x)
    out_specs, out_shape = _stream_out(m, d, tm, tn)
    return pl.pallas_call(
        _down_kernel,
        grid=(m // tm, d // tn),
        in_specs=in_specs + [pl.BlockSpec((1, tn), lambda i, j: (0, j))],
        out_specs=out_specs,
        out_shape=out_shape,
        compiler_params=_params("parallel", "arbitrary"),
        name="ffn_down",
    )(h, w_down, x, g_next.reshape(1, d))


def _gate_up_kernel(xg_ref, ssq_ref, wg_ref, wu_ref, o_ref, *, sub):
    wg = wg_ref[...].astype(_BF16)
    wu = wu_ref[...].astype(_BF16)
    rs_all = _row_scale(ssq_ref, xg_ref.shape[1])
    for r in range(xg_ref.shape[0] // sub):
        sl = slice(r * sub, (r + 1) * sub)
        xg = xg_ref[sl, :]
        rs = rs_all[sl]
        g = _dot(xg, wg) * rs
        u = _dot(xg, wu) * rs
        o_ref[sl, :] = ((g / (1.0 + jnp.exp(-g))) * u).astype(o_ref.dtype)


def _gate_up(xg, ssq, w_gate, w_up, layer, tm=2048, tf=256, sub=512):
    m, d = xg.shape
    f = w_gate.shape[2]
    return pl.pallas_call(
        functools.partial(_gate_up_kernel, sub=sub),
        grid=(m // tm, f // tf),
        in_specs=[pl.BlockSpec((tm, d), lambda i, j: (i, 0)),
                  pl.BlockSpec((tm, LANES), lambda i, j: (i, 0)),
                  pl.BlockSpec((None, d, tf), lambda i, j: (layer, 0, j)),
                  pl.BlockSpec((None, d, tf), lambda i, j: (layer, 0, j))],
        out_specs=pl.BlockSpec((tm, tf), lambda i, j: (i, j)),
        out_shape=jax.ShapeDtypeStruct((m, f), _BF16),
        compiler_params=_params("parallel", "parallel", vmem=VMEM_LIMIT_GATE_UP),
        name="ffn_gate_up",
    )(xg, ssq, w_gate, w_up)


def kernel(x, mix_norm, w_in, conv_w, gla_a_up, gla_a_bias, gla_norm, w_out,
           ffn_norm, w_gate, w_up, w_down, final_norm):
    batch, seq, d = x.shape
    depth = w_in.shape[0]
    xf = x.reshape(batch * seq, d)

    w_in_b = _cast_w_in(w_in, OFF_A)
    w_a = jnp.pad(w_in[:, :, OFF_A:], ((0, 0), (0, 0), (0, LANES - GLA_GATE_RANK))).astype(_BF16)
    w_out_b = w_out.astype(_BF16)
    w_down_b = w_down.astype(_BF16)
    a_up = jnp.pad(gla_a_up, ((0, 0), (0, LANES - GLA_GATE_RANK), (0, 0)))
    a_bias = gla_a_bias.reshape(depth, 1, -1)
    norm_g = gla_norm.reshape(depth, 1, -1)

    xg, ssq = _prep(xf, mix_norm[0])
    for l in range(depth):
        y_conv = _conv_mixer(xg, ssq, w_in_b, conv_w, l, seq)
        a_low = _gate_low(xg, ssq, w_a, l)
        y_gla = _gla_mixer(xg, ssq, a_low, w_in_b, a_up, a_bias, norm_g, l, seq)
        xf, xg, ssq = _out_proj(y_conv, y_gla, w_out_b, xf, ffn_norm[l], l)
        h = _gate_up(xg, ssq, w_gate, w_up, l)
        if l + 1 < depth:
            xf, xg, ssq = _down_proj(h, w_down_b, xf, mix_norm[l + 1], l)
        else:
            xf = _down_proj(h, w_down_b, xf, None, l)

    return _rmsnorm(xf, final_norm, x.dtype).reshape(batch, seq, d)
```

```python
import functools

import jax
import jax.numpy as jnp
from jax import lax
from jax.experimental import pallas as pl
from jax.experimental.pallas import tpu as pltpu

CHUNK = 64
CONV_WIDTH = 2048
CONV_K = 3
GLA_WIDTH = 2048
GLA_HEADS = 4
GLA_DK = 256
GLA_DV = 512
GLA_GATE_RANK = 16
GLA_GATE_TAU = 16.0
NORM_EPS = 1e-6

OFF_Q = 3 * CONV_WIDTH
OFF_K = OFF_Q + GLA_HEADS * GLA_DK
OFF_V = OFF_K + GLA_HEADS * GLA_DK
OFF_G = OFF_V + GLA_WIDTH
OFF_A = OFF_G + GLA_WIDTH

LANES = 128
SUBLANES = 8
GLA_SUB = 256
GLA_COLS = 2 * GLA_DK + 2 * GLA_DV
VMEM_LIMIT = 56 * 1024 * 1024
VMEM_LIMIT_GATE_UP = 60 * 1024 * 1024

_BF16 = jnp.bfloat16
_F32 = jnp.float32


def _params(*sem, vmem=VMEM_LIMIT):
    return pltpu.CompilerParams(dimension_semantics=sem, vmem_limit_bytes=vmem)


def _dot(a, b):
    return jnp.dot(a, b, preferred_element_type=_F32)


def _dot_nt(a, b):
    return lax.dot_general(a, b, (((1,), (1,)), ((), ())), preferred_element_type=_F32)


def _dot_tn(a, b):
    return lax.dot_general(a, b, (((0,), (0,)), ((), ())), preferred_element_type=_F32)


def _lane_tile_sum(v):
    acc = v[:, 0:LANES]
    for c in range(1, v.shape[1] // LANES):
        acc = acc + v[:, c * LANES:(c + 1) * LANES]
    return acc


def _row_scale(ssq_ref, d):
    return lax.rsqrt(jnp.sum(ssq_ref[...], axis=-1, keepdims=True) * (1.0 / d) + NORM_EPS)


def _split2(x):
    hi = x.astype(_BF16)
    return hi, (x - hi.astype(_F32)).astype(_BF16)


def _prep_kernel(x_ref, g_ref, xg_ref, ssq_ref):
    x = x_ref[...]
    xg_ref[...] = (x * g_ref[...]).astype(xg_ref.dtype)
    ssq_ref[...] = _lane_tile_sum(x * x)


def _prep(x, g, tm=256):
    m, d = x.shape
    return pl.pallas_call(
        _prep_kernel,
        grid=(m // tm,),
        in_specs=[pl.BlockSpec((tm, d), lambda i: (i, 0)),
                  pl.BlockSpec((1, d), lambda i: (0, 0))],
        out_specs=[pl.BlockSpec((tm, d), lambda i: (i, 0)),
                   pl.BlockSpec((tm, LANES), lambda i: (i, 0))],
        out_shape=[jax.ShapeDtypeStruct((m, d), _BF16),
                   jax.ShapeDtypeStruct((m, LANES), _F32)],
        compiler_params=_params("parallel"),
        name="prep",
    )(x, g.reshape(1, d))


def _rmsnorm_kernel(x_ref, g_ref, o_ref):
    x = x_ref[...]
    ms = jnp.mean(x * x, axis=-1, keepdims=True)
    o_ref[...] = (x * lax.rsqrt(ms + NORM_EPS) * g_ref[...]).astype(o_ref.dtype)


def _rmsnorm(x, g, out_dtype, tm=256):
    m, d = x.shape
    return pl.pallas_call(
        _rmsnorm_kernel,
        grid=(m // tm,),
        in_specs=[pl.BlockSpec((tm, d), lambda i: (i, 0)),
                  pl.BlockSpec((1, d), lambda i: (0, 0))],
        out_specs=pl.BlockSpec((tm, d), lambda i: (i, 0)),
        out_shape=jax.ShapeDtypeStruct((m, d), out_dtype),
        compiler_params=_params("parallel"),
        name="rmsnorm",
    )(x, g.reshape(1, d))


def _conv_kernel(xg_ref, ssq_ref, wb_ref, wc_ref, wh_ref, cw_ref, o_ref, carry_ref, *, tiles_per_seq):
    i = pl.program_id(1)

    @pl.when(i % tiles_per_seq == 0)
    def _():
        carry_ref[...] = jnp.zeros_like(carry_ref)

    tm, tn = o_ref.shape
    xg = xg_ref[...]
    rs = _row_scale(ssq_ref, xg_ref.shape[1])
    u = (_dot(xg, wc_ref[...]) * rs) * (_dot(xg, wh_ref[...]) * rs)
    row = lax.broadcasted_iota(jnp.int32, u.shape, 0)
    prev = carry_ref[...]
    p1 = prev[SUBLANES - 1:SUBLANES, :]
    p2 = prev[SUBLANES - 2:SUBLANES - 1, :]
    u1 = jnp.where(row == 0, p1, pltpu.roll(u, 1, axis=0))
    u2 = jnp.where(row == 0, p2, jnp.where(row == 1, p1, pltpu.roll(u, 2, axis=0)))
    carry_ref[...] = u[tm - SUBLANES:, :]
    cw = cw_ref[...]
    y = cw[0:1, :] * u2 + cw[1:2, :] * u1 + cw[2:3, :] * u
    o_ref[...] = (_dot(xg, wb_ref[...]) * rs * y).astype(o_ref.dtype)


def _conv_mixer(xg, ssq, w_in, conv_w, layer, seq, tm=1024, tn=512):
    m, d = xg.shape
    nb = CONV_WIDTH // tn
    kern = functools.partial(_conv_kernel, tiles_per_seq=seq // tm)
    wspec = lambda group: pl.BlockSpec((None, d, tn), lambda j, i: (layer, 0, group * nb + j))
    return pl.pallas_call(
        kern,
        grid=(nb, m // tm),
        in_specs=[pl.BlockSpec((tm, d), lambda j, i: (i, 0)),
                  pl.BlockSpec((tm, LANES), lambda j, i: (i, 0)),
                  wspec(0), wspec(1), wspec(2),
                  pl.BlockSpec((None, CONV_K, tn), lambda j, i: (layer, 0, j))],
        out_specs=pl.BlockSpec((tm, tn), lambda j, i: (i, j)),
        out_shape=jax.ShapeDtypeStruct((m, CONV_WIDTH), _BF16),
        scratch_shapes=[pltpu.VMEM((SUBLANES, tn), _F32)],
        compiler_params=_params("parallel", "arbitrary"),
        name="conv_mixer",
    )(xg, ssq, w_in, w_in, w_in, conv_w)


def _gate_low_kernel(xg_ref, ssq_ref, wa_ref, o_ref):
    o_ref[...] = _dot(xg_ref[...], wa_ref[...]) * _row_scale(ssq_ref, xg_ref.shape[1])


def _gate_low(xg, ssq, w_a, layer, tm=1024):
    m, d = xg.shape
    return pl.pallas_call(
        _gate_low_kernel,
        grid=(m // tm,),
        in_specs=[pl.BlockSpec((tm, d), lambda i: (i, 0)),
                  pl.BlockSpec((tm, LANES), lambda i: (i, 0)),
                  pl.BlockSpec((None, d, LANES), lambda i: (layer, 0, 0))],
        out_specs=pl.BlockSpec((tm, LANES), lambda i: (i, 0)),
        out_shape=jax.ShapeDtypeStruct((m, LANES), _F32),
        compiler_params=_params("parallel"),
        name="gate_low",
    )(xg, ssq, w_a)


def _gla_project_pieces(xg_ref, ssq_ref, w_refs, p_ref):
    d = xg_ref.shape[1]
    pieces = []
    for half in range(xg_ref.shape[0] // GLA_SUB):
        r = slice(half * GLA_SUB, (half + 1) * GLA_SUB)
        col = 0
        for w_ref in w_refs:
            width = w_ref.shape[1]

            def piece(r=r, col=col, width=width, w_ref=w_ref):
                rs = _row_scale(ssq_ref[r, :], d)
                p_ref[r, col:col + width] = _dot(xg_ref[r, :], w_ref[...]) * rs
            pieces.append(piece)
            col += width
    return pieces


def _gla_math(p_ref, a_ref, aup_ref, ab_ref, ng_ref, o_ref, st_ref, first_tile):
    rows = p_ref.shape[0]
    nchunk = GLA_SUB // CHUNK
    ri = lax.broadcasted_iota(jnp.int32, (GLA_SUB, GLA_SUB), 0)
    ci = lax.broadcasted_iota(jnp.int32, (GLA_SUB, GLA_SUB), 1)
    lower = (((ri // CHUNK) == (ci // CHUNK)) & (ci <= ri)).astype(_BF16)
    c_k, c_v, c_g = GLA_DK, 2 * GLA_DK, 2 * GLA_DK + GLA_DV
    chunk = lambda c: slice(c * CHUNK, (c + 1) * CHUNK)

    a_hi, a_lo = _split2(a_ref[...])
    w_hi, w_lo = _split2(aup_ref[...])
    z = ab_ref[...] + _dot(a_hi, w_hi) + _dot(a_lo, w_hi) + _dot(a_hi, w_lo)
    log_a = (jnp.minimum(z, 0.0) - jnp.log1p(jnp.exp(-jnp.abs(z)))) * (1.0 / GLA_GATE_TAU)
    la_hi, la_lo = _split2(log_a)
    yield

    state = jnp.where(first_tile, 0.0, st_ref[...])
    for s in range(rows // GLA_SUB):
        r = slice(s * GLA_SUB, (s + 1) * GLA_SUB)
        cum = _dot(lower, la_hi[r]) + _dot(lower, la_lo[r])
        totals = [cum[(c + 1) * CHUNK - 1:(c + 1) * CHUNK, :] for c in range(nchunk)]
        before = [jnp.zeros_like(totals[0])]
        for c in range(nchunk):
            before.append(before[c] + totals[c])
        total_b = jnp.concatenate([jnp.broadcast_to(tr, (CHUNK, GLA_DK)) for tr in totals], axis=0)
        k = p_ref[r, c_k:c_v]
        q_dec = p_ref[r, 0:c_k] * (GLA_DK ** -0.5) * jnp.exp(cum)
        k_inv = (k * jnp.exp(-cum)).astype(_BF16)
        k_dec = k * jnp.exp(total_b - cum)
        vb = p_ref[r, c_v:c_g].astype(_BF16)
        yield

        score_rows = []
        for c in range(nchunk):
            keys = [(k_dec[chunk(c2)] * jnp.exp(before[c] - before[c2 + 1])).astype(_BF16)
                    for c2 in range(c)]
            keys.append(k_inv[chunk(c)])
            if c + 1 < nchunk:
                keys.append(jnp.zeros(((nchunk - 1 - c) * CHUNK, GLA_DK), _BF16))
            score_rows.append(_dot_nt(q_dec[chunk(c)].astype(_BF16), jnp.concatenate(keys, axis=0)))
        scores = jnp.where(ci <= ri, jnp.concatenate(score_rows, axis=0), 0.0).astype(_BF16)
        yield

        q_state = jnp.concatenate([q_dec[chunk(c)] * jnp.exp(before[c]) for c in range(nchunk)], axis=0)
        o = _dot(scores, vb) + _dot_nt(q_state.astype(_BF16), state.astype(_BF16))
        k_end = jnp.concatenate([k_dec[chunk(c)] * jnp.exp(before[nchunk] - before[c + 1])
                                 for c in range(nchunk)], axis=0)
        yield
        state = state * jnp.exp(before[nchunk]) + _dot_tn(vb, k_end.astype(_BF16))
        yield

        o = o * lax.rsqrt(jnp.mean(o * o, axis=-1, keepdims=True) + NORM_EPS)
        g = p_ref[r, c_g:]
        o_ref[r, :] = (o * ng_ref[...] * (g / (1.0 + jnp.exp(-g)))).astype(o_ref.dtype)
        yield

    st_ref[...] = state


def _gla_kernel(xg_ref, ssq_ref, a_ref, wq_ref, wk_ref, wv_ref, wg_ref, aup_ref, ab_ref, ng_ref,
                o_ref, pa_ref, pb_ref, st_ref, *, tiles_per_seq):
    n = pl.program_id(1)
    w_refs = (wq_ref, wk_ref, wv_ref, wg_ref)

    @pl.when(n == 0)
    def _():
        pa_ref[...] = jnp.zeros_like(pa_ref)
        pb_ref[...] = jnp.zeros_like(pb_ref)
        st_ref[...] = jnp.zeros_like(st_ref)

    first_tile = (jnp.maximum(n - 1, 0) % tiles_per_seq) == 0

    def step(p_new, p_old):
        pieces = _gla_project_pieces(xg_ref, ssq_ref, w_refs, p_new)
        for _ in _gla_math(p_old, a_ref, aup_ref, ab_ref, ng_ref, o_ref, st_ref, first_tile):
            if pieces:
                pieces.pop(0)()
        for piece in pieces:
            piece()

    @pl.when(n % 2 == 0)
    def _():
        step(pa_ref, pb_ref)

    @pl.when(n % 2 == 1)
    def _():
        step(pb_ref, pa_ref)


def _gla_mixer(xg, ssq, a_low, w_in, a_up, a_bias, norm_g, layer, seq, tr=512):
    m, d = xg.shape
    nt = m // tr
    kern = functools.partial(_gla_kernel, tiles_per_seq=seq // tr)
    cur = lambda h, n: (jnp.minimum(n, nt - 1), 0)
    prev = lambda h, n: (jnp.maximum(n - 1, 0), 0)
    wspec = lambda off, width: pl.BlockSpec((None, d, width), lambda h, n: (layer, 0, off // width + h))
    return pl.pallas_call(
        kern,
        grid=(GLA_HEADS, nt + 1),
        in_specs=[pl.BlockSpec((tr, d), cur),
                  pl.BlockSpec((tr, LANES), cur),
                  pl.BlockSpec((tr, LANES), prev),
                  wspec(OFF_Q, GLA_DK), wspec(OFF_K, GLA_DK), wspec(OFF_V, GLA_DV), wspec(OFF_G, GLA_DV),
                  pl.BlockSpec((None, LANES, GLA_DK), lambda h, n: (layer, 0, h)),
                  pl.BlockSpec((None, 1, GLA_DK), lambda h, n: (layer, 0, h)),
                  pl.BlockSpec((None, 1, GLA_DV), lambda h, n: (layer, 0, h))],
        out_specs=pl.BlockSpec((tr, GLA_DV), lambda h, n: (jnp.maximum(n - 1, 0), h)),
        out_shape=jax.ShapeDtypeStruct((m, GLA_WIDTH), _BF16),
        scratch_shapes=[pltpu.VMEM((tr, GLA_COLS), _F32),
                        pltpu.VMEM((tr, GLA_COLS), _F32),
                        pltpu.VMEM((GLA_DV, GLA_DK), _F32)],
        compiler_params=_params("parallel", "arbitrary"),
        name="gla_mixer",
    )(xg, ssq, a_low, w_in, w_in, w_in, w_in, a_up, a_bias, norm_g)


def _init_ssq(ssq_ref):
    @pl.when(pl.program_id(1) == 0)
    def _():
        ssq_ref[...] = jnp.zeros_like(ssq_ref)


def _emit_stream(xn, gn_ref, o_ref, xg_ref, ssq_ref):
    o_ref[...] = xn
    xg_ref[...] = (xn * gn_ref[...]).astype(xg_ref.dtype)
    ssq_ref[...] += _lane_tile_sum(xn * xn)


def _out_kernel(yc_ref, yg_ref, w1_ref, w2_ref, x_ref, gn_ref, o_ref, xg_ref, ssq_ref):
    _init_ssq(ssq_ref)
    xn = x_ref[...] + _dot(yc_ref[...], w1_ref[...]) + _dot(yg_ref[...], w2_ref[...])
    _emit_stream(xn, gn_ref, o_ref, xg_ref, ssq_ref)


def _stream_out(m, d, tm, tn):
    specs = [pl.BlockSpec((tm, tn), lambda i, j: (i, j)),
             pl.BlockSpec((tm, tn), lambda i, j: (i, j)),
             pl.BlockSpec((tm, LANES), lambda i, j: (i, 0))]
    shapes = [jax.ShapeDtypeStruct((m, d), _F32),
              jax.ShapeDtypeStruct((m, d), _BF16),
              jax.ShapeDtypeStruct((m, LANES), _F32)]
    return specs, shapes


def _column_blocks(w, tn):
    depth, k, n = w.shape
    return w.astype(_BF16).reshape(depth, k, n // tn, tn).transpose(0, 2, 1, 3)


def _out_proj(y_conv, y_gla, w_out, x, g_next, layer, tm=1024):
    m, d = x.shape
    kc = y_conv.shape[1]
    kg = y_gla.shape[1]
    tn = w_out.shape[3]
    out_specs, out_shape = _stream_out(m, d, tm, tn)
    return pl.pallas_call(
        _out_kernel,
        grid=(m // tm, d // tn),
        in_specs=[pl.BlockSpec((tm, kc), lambda i, j: (i, 0)),
                  pl.BlockSpec((tm, kg), lambda i, j: (i, 0)),
                  pl.BlockSpec((None, None, kc, tn), lambda i, j: (layer, j, 0, 0)),
                  pl.BlockSpec((None, None, kg, tn), lambda i, j: (layer, j, 1, 0)),
                  pl.BlockSpec((tm, tn), lambda i, j: (i, j)),
                  pl.BlockSpec((1, tn), lambda i, j: (0, j))],
        out_specs=out_specs,
        out_shape=out_shape,
        compiler_params=_params("parallel", "arbitrary"),
        name="out_proj",
    )(y_conv, y_gla, w_out, w_out, x, g_next.reshape(1, d))


def _down_kernel(h_ref, w_ref, x_ref, gn_ref, o_ref, xg_ref, ssq_ref):
    _init_ssq(ssq_ref)
    xn = x_ref[...] + _dot(h_ref[...], w_ref[...])
    _emit_stream(xn, gn_ref, o_ref, xg_ref, ssq_ref)


def _down_last_kernel(h_ref, w_ref, x_ref, o_ref):
    o_ref[...] = x_ref[...] + _dot(h_ref[...], w_ref[...])


def _down_proj(h, w_down, x, g_next, layer, tm=512):
    m, d = x.shape
    f = h.shape[1]
    tn = w_down.shape[3]
    in_specs = [pl.BlockSpec((tm, f), lambda i, j: (i, 0)),
                pl.BlockSpec((None, None, f, tn), lambda i, j: (layer, j, 0, 0)),
                pl.BlockSpec((tm, tn), lambda i, j: (i, j))]
    if g_next is None:
        return pl.pallas_call(
            _down_last_kernel,
            grid=(m // tm, d // tn),
            in_specs=in_specs,
            out_specs=pl.BlockSpec((tm, tn), lambda i, j: (i, j)),
            out_shape=jax.ShapeDtypeStruct((m, d), _F32),
            compiler_params=_params("parallel", "parallel"),
            name="ffn_down_last",
        )(h, w_down, x)
    out_specs, out_shape = _stream_out(m, d, tm, tn)
    return pl.pallas_call(
        _down_kernel,
        grid=(m // tm, d // tn),
        in_specs=in_specs + [pl.BlockSpec((1, tn), lambda i, j: (0, j))],
        out_specs=out_specs,
        out_shape=out_shape,
        compiler_params=_params("parallel", "arbitrary"),
        name="ffn_down",
    )(h, w_down, x, g_next.reshape(1, d))


def _gate_up_kernel(xg_ref, ssq_ref, wg_ref, wu_ref, o_ref, *, sub):
    wg = wg_ref[...].astype(_BF16)
    wu = wu_ref[...].astype(_BF16)
    rs_all = _row_scale(ssq_ref, xg_ref.shape[1])
    for r in range(xg_ref.shape[0] // sub):
        sl = slice(r * sub, (r + 1) * sub)
        xg = xg_ref[sl, :]
        rs = rs_all[sl]
        g = _dot(xg, wg) * rs
        u = _dot(xg, wu) * rs
        o_ref[sl, :] = ((g / (1.0 + jnp.exp(-g))) * u).astype(o_ref.dtype)


def _gate_up(xg, ssq, w_gate, w_up, layer, tm=2048, tf=256, sub=512):
    m, d = xg.shape
    f = w_gate.shape[2]
    return pl.pallas_call(
        functools.partial(_gate_up_kernel, sub=sub),
        grid=(m // tm, f // tf),
        in_specs=[pl.BlockSpec((tm, d), lambda i, j: (i, 0)),
                  pl.BlockSpec((tm, LANES), lambda i, j: (i, 0)),
                  pl.BlockSpec((None, d, tf), lambda i, j: (layer, 0, j)),
                  pl.BlockSpec((None, d, tf), lambda i, j: (layer, 0, j))],
        out_specs=pl.BlockSpec((tm, tf), lambda i, j: (i, j)),
        out_shape=jax.ShapeDtypeStruct((m, f), _BF16),
        compiler_params=_params("parallel", "parallel", vmem=VMEM_LIMIT_GATE_UP),
        name="ffn_gate_up",
    )(xg, ssq, w_gate, w_up)


def kernel(x, mix_norm, w_in, conv_w, gla_a_up, gla_a_bias, gla_norm, w_out,
           ffn_norm, w_gate, w_up, w_down, final_norm):
    batch, seq, d = x.shape
    depth = w_in.shape[0]
    xf = x.reshape(batch * seq, d)

    w_in_b = w_in.astype(_BF16)
    w_a = jnp.pad(w_in[:, :, OFF_A:], ((0, 0), (0, 0), (0, LANES - GLA_GATE_RANK))).astype(_BF16)
    w_out_b = _column_blocks(w_out, 512)
    w_down_b = _column_blocks(w_down, 512)
    a_up = jnp.pad(gla_a_up, ((0, 0), (0, LANES - GLA_GATE_RANK), (0, 0)))
    a_bias = gla_a_bias.reshape(depth, 1, -1)
    norm_g = gla_norm.reshape(depth, 1, -1)

    xg, ssq = _prep(xf, mix_norm[0])
    for l in range(depth):
        y_conv = _conv_mixer(xg, ssq, w_in_b, conv_w, l, seq)
        a_low = _gate_low(xg, ssq, w_a, l)
        y_gla = _gla_mixer(xg, ssq, a_low, w_in_b, a_up, a_bias, norm_g, l, seq)
        xf, xg, ssq = _out_proj(y_conv, y_gla, w_out_b, xf, ffn_norm[l], l)
        h = _gate_up(xg, ssq, w_gate, w_up, l)
        if l + 1 < depth:
            xf, xg, ssq = _down_proj(h, w_down_b, xf, mix_norm[l + 1], l)
        else:
            xf = _down_proj(h, w_down_b, xf, None, l)

    return _rmsnorm(xf, final_norm, x.dtype).reshape(batch, seq, d)
```

```python
import functools

import jax
import jax.numpy as jnp
from jax import lax
from jax.experimental import pallas as pl
from jax.experimental.pallas import tpu as pltpu

CHUNK = 64
CONV_WIDTH = 2048
CONV_K = 3
GLA_WIDTH = 2048
GLA_HEADS = 4
GLA_DK = 256
GLA_DV = 512
GLA_GATE_RANK = 16
GLA_GATE_TAU = 16.0
NORM_EPS = 1e-6

OFF_Q = 3 * CONV_WIDTH
OFF_K = OFF_Q + GLA_HEADS * GLA_DK
OFF_V = OFF_K + GLA_HEADS * GLA_DK
OFF_G = OFF_V + GLA_WIDTH
OFF_A = OFF_G + GLA_WIDTH

LANES = 128
SUBLANES = 8
GLA_SUB = 256
GLA_COLS = 2 * GLA_DK + 2 * GLA_DV
VMEM_LIMIT = 56 * 1024 * 1024
VMEM_LIMIT_GATE_UP = 60 * 1024 * 1024

_BF16 = jnp.bfloat16
_F32 = jnp.float32


def _params(*sem, vmem=VMEM_LIMIT):
    return pltpu.CompilerParams(dimension_semantics=sem, vmem_limit_bytes=vmem)


def _dot(a, b):
    return jnp.dot(a, b, preferred_element_type=_F32)


def _dot_nt(a, b):
    return lax.dot_general(a, b, (((1,), (1,)), ((), ())), preferred_element_type=_F32)


def _dot_tn(a, b):
    return lax.dot_general(a, b, (((0,), (0,)), ((), ())), preferred_element_type=_F32)


def _lane_tile_sum(v):
    acc = v[:, 0:LANES]
    for c in range(1, v.shape[1] // LANES):
        acc = acc + v[:, c * LANES:(c + 1) * LANES]
    return acc


def _row_scale(ssq_ref, d):
    return lax.rsqrt(jnp.sum(ssq_ref[...], axis=-1, keepdims=True) * (1.0 / d) + NORM_EPS)


def _split2(x):
    hi = x.astype(_BF16)
    return hi, (x - hi.astype(_F32)).astype(_BF16)


def _prep_kernel(x_ref, g_ref, xg_ref, ssq_ref):
    x = x_ref[...]
    xg_ref[...] = (x * g_ref[...]).astype(xg_ref.dtype)
    ssq_ref[...] = _lane_tile_sum(x * x)


def _prep(x, g, tm=256):
    m, d = x.shape
    return pl.pallas_call(
        _prep_kernel,
        grid=(m // tm,),
        in_specs=[pl.BlockSpec((tm, d), lambda i: (i, 0)),
                  pl.BlockSpec((1, d), lambda i: (0, 0))],
        out_specs=[pl.BlockSpec((tm, d), lambda i: (i, 0)),
                   pl.BlockSpec((tm, LANES), lambda i: (i, 0))],
        out_shape=[jax.ShapeDtypeStruct((m, d), _BF16),
                   jax.ShapeDtypeStruct((m, LANES), _F32)],
        compiler_params=_params("parallel"),
        name="prep",
    )(x, g.reshape(1, d))


def _rmsnorm_kernel(x_ref, g_ref, o_ref):
    x = x_ref[...]
    ms = jnp.mean(x * x, axis=-1, keepdims=True)
    o_ref[...] = (x * lax.rsqrt(ms + NORM_EPS) * g_ref[...]).astype(o_ref.dtype)


def _rmsnorm(x, g, out_dtype, tm=256):
    m, d = x.shape
    return pl.pallas_call(
        _rmsnorm_kernel,
        grid=(m // tm,),
        in_specs=[pl.BlockSpec((tm, d), lambda i: (i, 0)),
                  pl.BlockSpec((1, d), lambda i: (0, 0))],
        out_specs=pl.BlockSpec((tm, d), lambda i: (i, 0)),
        out_shape=jax.ShapeDtypeStruct((m, d), out_dtype),
        compiler_params=_params("parallel"),
        name="rmsnorm",
    )(x, g.reshape(1, d))


def _conv_kernel(xg_ref, ssq_ref, wb_ref, wc_ref, wh_ref, cw_ref, o_ref, carry_ref, *, tiles_per_seq):
    i = pl.program_id(1)

    @pl.when(i % tiles_per_seq == 0)
    def _():
        carry_ref[...] = jnp.zeros_like(carry_ref)

    tm, tn = o_ref.shape
    xg = xg_ref[...]
    rs = _row_scale(ssq_ref, xg_ref.shape[1])
    u = (_dot(xg, wc_ref[...]) * rs) * (_dot(xg, wh_ref[...]) * rs)
    row = lax.broadcasted_iota(jnp.int32, u.shape, 0)
    prev = carry_ref[...]
    p1 = prev[SUBLANES - 1:SUBLANES, :]
    p2 = prev[SUBLANES - 2:SUBLANES - 1, :]
    u1 = jnp.where(row == 0, p1, pltpu.roll(u, 1, axis=0))
    u2 = jnp.where(row == 0, p2, jnp.where(row == 1, p1, pltpu.roll(u, 2, axis=0)))
    carry_ref[...] = u[tm - SUBLANES:, :]
    cw = cw_ref[...]
    y = cw[0:1, :] * u2 + cw[1:2, :] * u1 + cw[2:3, :] * u
    o_ref[...] = (_dot(xg, wb_ref[...]) * rs * y).astype(o_ref.dtype)


def _conv_mixer(xg, ssq, w_in, conv_w, layer, seq, tm=1024, tn=512):
    m, d = xg.shape
    nb = CONV_WIDTH // tn
    kern = functools.partial(_conv_kernel, tiles_per_seq=seq // tm)
    wspec = lambda group: pl.BlockSpec((None, d, tn), lambda j, i: (layer, 0, group * nb + j))
    return pl.pallas_call(
        kern,
        grid=(nb, m // tm),
        in_specs=[pl.BlockSpec((tm, d), lambda j, i: (i, 0)),
                  pl.BlockSpec((tm, LANES), lambda j, i: (i, 0)),
                  wspec(0), wspec(1), wspec(2),
                  pl.BlockSpec((None, CONV_K, tn), lambda j, i: (layer, 0, j))],
        out_specs=pl.BlockSpec((tm, tn), lambda j, i: (i, j)),
        out_shape=jax.ShapeDtypeStruct((m, CONV_WIDTH), _BF16),
        scratch_shapes=[pltpu.VMEM((SUBLANES, tn), _F32)],
        compiler_params=_params("parallel", "arbitrary"),
        name="conv_mixer",
    )(xg, ssq, w_in, w_in, w_in, conv_w)


def _gate_low_kernel(xg_ref, ssq_ref, wa_ref, o_ref):
    o_ref[...] = _dot(xg_ref[...], wa_ref[...]) * _row_scale(ssq_ref, xg_ref.shape[1])


def _gate_low(xg, ssq, w_a, layer, tm=1024):
    m, d = xg.shape
    return pl.pallas_call(
        _gate_low_kernel,
        grid=(m // tm,),
        in_specs=[pl.BlockSpec((tm, d), lambda i: (i, 0)),
                  pl.BlockSpec((tm, LANES), lambda i: (i, 0)),
                  pl.BlockSpec((None, d, LANES), lambda i: (layer, 0, 0))],
        out_specs=pl.BlockSpec((tm, LANES), lambda i: (i, 0)),
        out_shape=jax.ShapeDtypeStruct((m, LANES), _F32),
        compiler_params=_params("parallel"),
        name="gate_low",
    )(xg, ssq, w_a)


def _gla_project_pieces(xg_ref, ssq_ref, w_refs, p_ref):
    d = xg_ref.shape[1]
    pieces = []
    for half in range(xg_ref.shape[0] // GLA_SUB):
        r = slice(half * GLA_SUB, (half + 1) * GLA_SUB)
        col = 0
        for w_ref in w_refs:
            width = w_ref.shape[1]

            def piece(r=r, col=col, width=width, w_ref=w_ref):
                rs = _row_scale(ssq_ref[r, :], d)
                p_ref[r, col:col + width] = _dot(xg_ref[r, :], w_ref[...]) * rs
            pieces.append(piece)
            col += width
    return pieces


def _gla_math(p_ref, a_ref, aup_ref, ab_ref, ng_ref, o_ref, st_ref, first_tile):
    rows = p_ref.shape[0]
    nchunk = GLA_SUB // CHUNK
    ri = lax.broadcasted_iota(jnp.int32, (GLA_SUB, GLA_SUB), 0)
    ci = lax.broadcasted_iota(jnp.int32, (GLA_SUB, GLA_SUB), 1)
    lower = (((ri // CHUNK) == (ci // CHUNK)) & (ci <= ri)).astype(_BF16)
    c_k, c_v, c_g = GLA_DK, 2 * GLA_DK, 2 * GLA_DK + GLA_DV
    chunk = lambda c: slice(c * CHUNK, (c + 1) * CHUNK)

    a_hi, a_lo = _split2(a_ref[...])
    w_hi, w_lo = _split2(aup_ref[...])
    z = ab_ref[...] + _dot(a_hi, w_hi) + _dot(a_lo, w_hi) + _dot(a_hi, w_lo)
    log_a = (jnp.minimum(z, 0.0) - jnp.log1p(jnp.exp(-jnp.abs(z)))) * (1.0 / GLA_GATE_TAU)
    la_hi, la_lo = _split2(log_a)
    yield

    state = jnp.where(first_tile, 0.0, st_ref[...])
    for s in range(rows // GLA_SUB):
        r = slice(s * GLA_SUB, (s + 1) * GLA_SUB)
        cum = _dot(lower, la_hi[r]) + _dot(lower, la_lo[r])
        totals = [cum[(c + 1) * CHUNK - 1:(c + 1) * CHUNK, :] for c in range(nchunk)]
        before = [jnp.zeros_like(totals[0])]
        for c in range(nchunk):
            before.append(before[c] + totals[c])
        total_b = jnp.concatenate([jnp.broadcast_to(tr, (CHUNK, GLA_DK)) for tr in totals], axis=0)
        k = p_ref[r, c_k:c_v]
        q_dec = p_ref[r, 0:c_k] * (GLA_DK ** -0.5) * jnp.exp(cum)
        k_inv = (k * jnp.exp(-cum)).astype(_BF16)
        k_dec = k * jnp.exp(total_b - cum)
        vb = p_ref[r, c_v:c_g].astype(_BF16)
        yield

        score_rows = []
        for c in range(nchunk):
            keys = [(k_dec[chunk(c2)] * jnp.exp(before[c] - before[c2 + 1])).astype(_BF16)
                    for c2 in range(c)]
            keys.append(k_inv[chunk(c)])
            if c + 1 < nchunk:
                keys.append(jnp.zeros(((nchunk - 1 - c) * CHUNK, GLA_DK), _BF16))
            score_rows.append(_dot_nt(q_dec[chunk(c)].astype(_BF16), jnp.concatenate(keys, axis=0)))
        scores = jnp.where(ci <= ri, jnp.concatenate(score_rows, axis=0), 0.0).astype(_BF16)
        yield

        q_state = jnp.concatenate([q_dec[chunk(c)] * jnp.exp(before[c]) for c in range(nchunk)], axis=0)
        o = _dot(scores, vb) + _dot_nt(q_state.astype(_BF16), state.astype(_BF16))
        k_end = jnp.concatenate([k_dec[chunk(c)] * jnp.exp(before[nchunk] - before[c + 1])
                                 for c in range(nchunk)], axis=0)
        yield
        state = state * jnp.exp(before[nchunk]) + _dot_tn(vb, k_end.astype(_BF16))
        yield

        o = o * lax.rsqrt(jnp.mean(o * o, axis=-1, keepdims=True) + NORM_EPS)
        g = p_ref[r, c_g:]
        o_ref[r, :] = (o * ng_ref[...] * (g / (1.0 + jnp.exp(-g)))).astype(o_ref.dtype)
        yield

    st_ref[...] = state


def _gla_kernel(xg_ref, ssq_ref, a_ref, wq_ref, wk_ref, wv_ref, wg_ref, aup_ref, ab_ref, ng_ref,
                o_ref, pa_ref, pb_ref, st_ref, *, tiles_per_seq):
    n = pl.program_id(1)
    w_refs = (wq_ref, wk_ref, wv_ref, wg_ref)

    @pl.when(n == 0)
    def _():
        pa_ref[...] = jnp.zeros_like(pa_ref)
        pb_ref[...] = jnp.zeros_like(pb_ref)
        st_ref[...] = jnp.zeros_like(st_ref)

    first_tile = (jnp.maximum(n - 1, 0) % tiles_per_seq) == 0

    def step(p_new, p_old):
        pieces = _gla_project_pieces(xg_ref, ssq_ref, w_refs, p_new)
        for _ in _gla_math(p_old, a_ref, aup_ref, ab_ref, ng_ref, o_ref, st_ref, first_tile):
            if pieces:
                pieces.pop(0)()
        for piece in pieces:
            piece()

    @pl.when(n % 2 == 0)
    def _():
        step(pa_ref, pb_ref)

    @pl.when(n % 2 == 1)
    def _():
        step(pb_ref, pa_ref)


def _gla_mixer(xg, ssq, a_low, w_in, a_up, a_bias, norm_g, layer, seq, tr=512):
    m, d = xg.shape
    nt = m // tr
    kern = functools.partial(_gla_kernel, tiles_per_seq=seq // tr)
    cur = lambda h, n: (jnp.minimum(n, nt - 1), 0)
    prev = lambda h, n: (jnp.maximum(n - 1, 0), 0)
    wspec = lambda off, width: pl.BlockSpec((None, d, width), lambda h, n: (layer, 0, off // width + h))
    return pl.pallas_call(
        kern,
        grid=(GLA_HEADS, nt + 1),
        in_specs=[pl.BlockSpec((tr, d), cur),
                  pl.BlockSpec((tr, LANES), cur),
                  pl.BlockSpec((tr, LANES), prev),
                  wspec(OFF_Q, GLA_DK), wspec(OFF_K, GLA_DK), wspec(OFF_V, GLA_DV), wspec(OFF_G, GLA_DV),
                  pl.BlockSpec((None, LANES, GLA_DK), lambda h, n: (layer, 0, h)),
                  pl.BlockSpec((None, 1, GLA_DK), lambda h, n: (layer, 0, h)),
                  pl.BlockSpec((None, 1, GLA_DV), lambda h, n: (layer, 0, h))],
        out_specs=pl.BlockSpec((tr, GLA_DV), lambda h, n: (jnp.maximum(n - 1, 0), h)),
        out_shape=jax.ShapeDtypeStruct((m, GLA_WIDTH), _BF16),
        scratch_shapes=[pltpu.VMEM((tr, GLA_COLS), _F32),
                        pltpu.VMEM((tr, GLA_COLS), _F32),
                        pltpu.VMEM((GLA_DV, GLA_DK), _F32)],
        compiler_params=_params("parallel", "arbitrary"),
        name="gla_mixer",
    )(xg, ssq, a_low, w_in, w_in, w_in, w_in, a_up, a_bias, norm_g)


def _init_ssq(ssq_ref):
    @pl.when(pl.program_id(1) == 0)
    def _():
        ssq_ref[...] = jnp.zeros_like(ssq_ref)


def _emit_stream(xn, gn_ref, o_ref, xg_ref, ssq_ref):
    o_ref[...] = xn
    xg_ref[...] = (xn * gn_ref[...]).astype(xg_ref.dtype)
    ssq_ref[...] += _lane_tile_sum(xn * xn)


def _out_kernel(yc_ref, yg_ref, w1_ref, w2_ref, x_ref, gn_ref, o_ref, xg_ref, ssq_ref):
    _init_ssq(ssq_ref)
    xn = x_ref[...] + _dot(yc_ref[...], w1_ref[...]) + _dot(yg_ref[...], w2_ref[...])
    _emit_stream(xn, gn_ref, o_ref, xg_ref, ssq_ref)


def _stream_out(m, d, tm, tn):
    specs = [pl.BlockSpec((tm, tn), lambda i, j: (i, j)),
             pl.BlockSpec((tm, tn), lambda i, j: (i, j)),
             pl.BlockSpec((tm, LANES), lambda i, j: (i, 0))]
    shapes = [jax.ShapeDtypeStruct((m, d), _F32),
              jax.ShapeDtypeStruct((m, d), _BF16),
              jax.ShapeDtypeStruct((m, LANES), _F32)]
    return specs, shapes


def _out_proj(y_conv, y_gla, w_out, x, g_next, layer, tm=1024, tn=512):
    m, d = x.shape
    kc = y_conv.shape[1]
    kg = y_gla.shape[1]
    out_specs, out_shape = _stream_out(m, d, tm, tn)
    return pl.pallas_call(
        _out_kernel,
        grid=(m // tm, d // tn),
        in_specs=[pl.BlockSpec((tm, kc), lambda i, j: (i, 0)),
                  pl.BlockSpec((tm, kg), lambda i, j: (i, 0)),
                  pl.BlockSpec((None, kc, tn), lambda i, j: (layer, 0, j)),
                  pl.BlockSpec((None, kg, tn), lambda i, j: (layer, 1, j)),
                  pl.BlockSpec((tm, tn), lambda i, j: (i, j)),
                  pl.BlockSpec((1, tn), lambda i, j: (0, j))],
        out_specs=out_specs,
        out_shape=out_shape,
        compiler_params=_params("parallel", "arbitrary"),
        name="out_proj",
    )(y_conv, y_gla, w_out, w_out, x, g_next.reshape(1, d))


def _down_kernel(h_ref, w_ref, x_ref, gn_ref, o_ref, xg_ref, ssq_ref):
    _init_ssq(ssq_ref)
    xn = x_ref[...] + _dot(h_ref[...], w_ref[...])
    _emit_stream(xn, gn_ref, o_ref, xg_ref, ssq_ref)


def _down_last_kernel(h_ref, w_ref, x_ref, o_ref):
    o_ref[...] = x_ref[...] + _dot(h_ref[...], w_ref[...])


def _down_proj(h, w_down, x, g_next, layer, tm=512, tn=512):
    m, d = x.shape
    f = h.shape[1]
    in_specs = [pl.BlockSpec((tm, f), lambda i, j: (i, 0)),
                pl.BlockSpec((None, f, tn), lambda i, j: (layer, 0, j)),
                pl.BlockSpec((tm, tn), lambda i, j: (i, j))]
    if g_next is None:
        return pl.pallas_call(
            _down_last_kernel,
            grid=(m // tm, d // tn),
            in_specs=in_specs,
            out_specs=pl.BlockSpec((tm, tn), lambda i, j: (i, j)),
            out_shape=jax.ShapeDtypeStruct((m, d), _F32),
            compiler_params=_params("parallel", "parallel"),
            name="ffn_down_last",
        )(h, w_down, x)
    out_specs, out_shape = _stream_out(m, d, tm, tn)
    return pl.pallas_call(
        _down_kernel,
        grid=(m // tm, d // tn),
        in_specs=in_specs + [pl.BlockSpec((1, tn), lambda i, j: (0, j))],
        out_specs=out_specs,
        out_shape=out_shape,
        compiler_params=_params("parallel", "arbitrary"),
        name="ffn_down",
    )(h, w_down, x, g_next.reshape(1, d))


def _gate_up_kernel(xg_ref, ssq_ref, wg_ref, wu_ref, o_ref, *, sub):
    wg = wg_ref[...].astype(_BF16)
    wu = wu_ref[...].astype(_BF16)
    rs_all = _row_scale(ssq_ref, xg_ref.shape[1])
    for r in range(xg_ref.shape[0] // sub):
        sl = slice(r * sub, (r + 1) * sub)
        xg = xg_ref[sl, :]
        rs = rs_all[sl]
        g = _dot(xg, wg) * rs
        u = _dot(xg, wu) * rs
        o_ref[sl, :] = ((g / (1.0 + jnp.exp(-g))) * u).astype(o_ref.dtype)


def _gate_up(xg, ssq, w_gate, w_up, layer, tm=2048, tf=256, sub=512):
    m, d = xg.shape
    f = w_gate.shape[2]
    return pl.pallas_call(
        functools.partial(_gate_up_kernel, sub=sub),
        grid=(m // tm, f // tf),
        in_specs=[pl.BlockSpec((tm, d), lambda i, j: (i, 0)),
                  pl.BlockSpec((tm, LANES), lambda i, j: (i, 0)),
                  pl.BlockSpec((None, d, tf), lambda i, j: (layer, 0, j)),
                  pl.BlockSpec((None, d, tf), lambda i, j: (layer, 0, j))],
        out_specs=pl.BlockSpec((tm, tf), lambda i, j: (i, j)),
        out_shape=jax.ShapeDtypeStruct((m, f), _BF16),
        compiler_params=_params("parallel", "parallel", vmem=VMEM_LIMIT_GATE_UP),
        name="ffn_gate_up",
    )(xg, ssq, w_gate, w_up)


def kernel(x, mix_norm, w_in, conv_w, gla_a_up, gla_a_bias, gla_norm, w_out,
           ffn_norm, w_gate, w_up, w_down, final_norm):
    batch, seq, d = x.shape
    depth = w_in.shape[0]
    xf = x.reshape(batch * seq, d)

    w_in_b = w_in.astype(_BF16)
    w_a = jnp.pad(w_in[:, :, OFF_A:], ((0, 0), (0, 0), (0, LANES - GLA_GATE_RANK))).astype(_BF16)
    w_out_b = w_out.astype(_BF16)
    w_down_b = w_down.astype(_BF16)
    a_up = jnp.pad(gla_a_up, ((0, 0), (0, LANES - GLA_GATE_RANK), (0, 0)))
    a_bias = gla_a_bias.reshape(depth, 1, -1)
    norm_g = gla_norm.reshape(depth, 1, -1)

    xg, ssq = _prep(xf, mix_norm[0])
    for l in range(depth):
        y_conv = _conv_mixer(xg, ssq, w_in_b, conv_w, l, seq)
        a_low = _gate_low(xg, ssq, w_a, l)
        y_gla = _gla_mixer(xg, ssq, a_low, w_in_b, a_up, a_bias, norm_g, l, seq)
        xf, xg, ssq = _out_proj(y_conv, y_gla, w_out_b, xf, ffn_norm[l], l)
        h = _gate_up(xg, ssq, w_gate, w_up, l)
        if l + 1 < depth:
            xf, xg, ssq = _down_proj(h, w_down_b, xf, mix_norm[l + 1], l)
        else:
            xf = _down_proj(h, w_down_b, xf, None, l)

    return _rmsnorm(xf, final_norm, x.dtype).reshape(batch, seq, d)
```

```python
import functools

import jax
import jax.numpy as jnp
from jax import lax
from jax.experimental import pallas as pl
from jax.experimental.pallas import tpu as pltpu

CHUNK = 64
CONV_WIDTH = 2048
CONV_K = 3
GLA_WIDTH = 2048
GLA_HEADS = 4
GLA_DK = 256
GLA_DV = 512
GLA_GATE_RANK = 16
GLA_GATE_TAU = 16.0
NORM_EPS = 1e-6

OFF_Q = 3 * CONV_WIDTH
OFF_K = OFF_Q + GLA_HEADS * GLA_DK
OFF_V = OFF_K + GLA_HEADS * GLA_DK
OFF_G = OFF_V + GLA_WIDTH
OFF_A = OFF_G + GLA_WIDTH

LANES = 128
SUBLANES = 8
GLA_SUB = 256
GLA_COLS = 2 * GLA_DK + 2 * GLA_DV
VMEM_LIMIT = 56 * 1024 * 1024
VMEM_LIMIT_GATE_UP = 60 * 1024 * 1024

_BF16 = jnp.bfloat16
_F32 = jnp.float32


def _params(*sem, vmem=VMEM_LIMIT):
    return pltpu.CompilerParams(dimension_semantics=sem, vmem_limit_bytes=vmem)


def _dot(a, b):
    return jnp.dot(a, b, preferred_element_type=_F32)


def _dot_nt(a, b):
    return lax.dot_general(a, b, (((1,), (1,)), ((), ())), preferred_element_type=_F32)


def _dot_tn(a, b):
    return lax.dot_general(a, b, (((0,), (0,)), ((), ())), preferred_element_type=_F32)


def _lane_tile_sum(v):
    acc = v[:, 0:LANES]
    for c in range(1, v.shape[1] // LANES):
        acc = acc + v[:, c * LANES:(c + 1) * LANES]
    return acc


def _row_scale(ssq_ref, d):
    return lax.rsqrt(jnp.sum(ssq_ref[...], axis=-1, keepdims=True) * (1.0 / d) + NORM_EPS)


def _split2(x):
    hi = x.astype(_BF16)
    return hi, (x - hi.astype(_F32)).astype(_BF16)


def _prep_kernel(x_ref, g_ref, xg_ref, ssq_ref):
    x = x_ref[...]
    xg_ref[...] = (x * g_ref[...]).astype(xg_ref.dtype)
    ssq_ref[...] = _lane_tile_sum(x * x)


def _prep(x, g, tm=256):
    m, d = x.shape
    return pl.pallas_call(
        _prep_kernel,
        grid=(m // tm,),
        in_specs=[pl.BlockSpec((tm, d), lambda i: (i, 0)),
                  pl.BlockSpec((1, d), lambda i: (0, 0))],
        out_specs=[pl.BlockSpec((tm, d), lambda i: (i, 0)),
                   pl.BlockSpec((tm, LANES), lambda i: (i, 0))],
        out_shape=[jax.ShapeDtypeStruct((m, d), _BF16),
                   jax.ShapeDtypeStruct((m, LANES), _F32)],
        compiler_params=_params("parallel"),
        name="prep",
    )(x, g.reshape(1, d))


def _rmsnorm_kernel(x_ref, g_ref, o_ref):
    x = x_ref[...]
    ms = jnp.mean(x * x, axis=-1, keepdims=True)
    o_ref[...] = (x * lax.rsqrt(ms + NORM_EPS) * g_ref[...]).astype(o_ref.dtype)


def _rmsnorm(x, g, out_dtype, tm=256):
    m, d = x.shape
    return pl.pallas_call(
        _rmsnorm_kernel,
        grid=(m // tm,),
        in_specs=[pl.BlockSpec((tm, d), lambda i: (i, 0)),
                  pl.BlockSpec((1, d), lambda i: (0, 0))],
        out_specs=pl.BlockSpec((tm, d), lambda i: (i, 0)),
        out_shape=jax.ShapeDtypeStruct((m, d), out_dtype),
        compiler_params=_params("parallel"),
        name="rmsnorm",
    )(x, g.reshape(1, d))


def _conv_kernel(xg_ref, ssq_ref, wb_ref, wc_ref, wh_ref, cw_ref, o_ref, carry_ref, *, tiles_per_seq):
    i = pl.program_id(1)

    @pl.when(i % tiles_per_seq == 0)
    def _():
        carry_ref[...] = jnp.zeros_like(carry_ref)

    tm, tn = o_ref.shape
    xg = xg_ref[...]
    rs = _row_scale(ssq_ref, xg_ref.shape[1])
    u = (_dot(xg, wc_ref[...]) * rs) * (_dot(xg, wh_ref[...]) * rs)
    row = lax.broadcasted_iota(jnp.int32, u.shape, 0)
    prev = carry_ref[...]
    p1 = prev[SUBLANES - 1:SUBLANES, :]
    p2 = prev[SUBLANES - 2:SUBLANES - 1, :]
    u1 = jnp.where(row == 0, p1, pltpu.roll(u, 1, axis=0))
    u2 = jnp.where(row == 0, p2, jnp.where(row == 1, p1, pltpu.roll(u, 2, axis=0)))
    carry_ref[...] = u[tm - SUBLANES:, :]
    cw = cw_ref[...]
    y = cw[0:1, :] * u2 + cw[1:2, :] * u1 + cw[2:3, :] * u
    o_ref[...] = (_dot(xg, wb_ref[...]) * rs * y).astype(o_ref.dtype)


def _conv_mixer(xg, ssq, w_in, conv_w, layer, seq, tm=1024, tn=512):
    m, d = xg.shape
    nb = CONV_WIDTH // tn
    kern = functools.partial(_conv_kernel, tiles_per_seq=seq // tm)
    wspec = lambda group: pl.BlockSpec((None, d, tn), lambda j, i: (layer, 0, group * nb + j))
    return pl.pallas_call(
        kern,
        grid=(nb, m // tm),
        in_specs=[pl.BlockSpec((tm, d), lambda j, i: (i, 0)),
                  pl.BlockSpec((tm, LANES), lambda j, i: (i, 0)),
                  wspec(0), wspec(1), wspec(2),
                  pl.BlockSpec((None, CONV_K, tn), lambda j, i: (layer, 0, j))],
        out_specs=pl.BlockSpec((tm, tn), lambda j, i: (i, j)),
        out_shape=jax.ShapeDtypeStruct((m, CONV_WIDTH), _BF16),
        scratch_shapes=[pltpu.VMEM((SUBLANES, tn), _F32)],
        compiler_params=_params("parallel", "arbitrary"),
        name="conv_mixer",
    )(xg, ssq, w_in, w_in, w_in, conv_w)


def _gate_low_kernel(xg_ref, ssq_ref, wa_ref, o_ref):
    o_ref[...] = _dot(xg_ref[...], wa_ref[...]) * _row_scale(ssq_ref, xg_ref.shape[1])


def _gate_low(xg, ssq, w_a, layer, tm=1024):
    m, d = xg.shape
    return pl.pallas_call(
        _gate_low_kernel,
        grid=(m // tm,),
        in_specs=[pl.BlockSpec((tm, d), lambda i: (i, 0)),
                  pl.BlockSpec((tm, LANES), lambda i: (i, 0)),
                  pl.BlockSpec((None, d, LANES), lambda i: (layer, 0, 0))],
        out_specs=pl.BlockSpec((tm, LANES), lambda i: (i, 0)),
        out_shape=jax.ShapeDtypeStruct((m, LANES), _F32),
        compiler_params=_params("parallel"),
        name="gate_low",
    )(xg, ssq, w_a)


def _gla_project_pieces(xg_ref, ssq_ref, w_refs, p_ref):
    d = xg_ref.shape[1]
    pieces = []
    for half in range(xg_ref.shape[0] // GLA_SUB):
        r = slice(half * GLA_SUB, (half + 1) * GLA_SUB)
        col = 0
        for w_ref in w_refs:
            width = w_ref.shape[1]

            def piece(r=r, col=col, width=width, w_ref=w_ref):
                rs = _row_scale(ssq_ref[r, :], d)
                p_ref[r, col:col + width] = _dot(xg_ref[r, :], w_ref[...]) * rs
            pieces.append(piece)
            col += width
    return pieces


def _gla_math(p_ref, a_ref, aup_ref, ab_ref, ng_ref, o_ref, st_ref, first_tile):
    rows = p_ref.shape[0]
    nchunk = GLA_SUB // CHUNK
    ri = lax.broadcasted_iota(jnp.int32, (GLA_SUB, GLA_SUB), 0)
    ci = lax.broadcasted_iota(jnp.int32, (GLA_SUB, GLA_SUB), 1)
    lower = (((ri // CHUNK) == (ci // CHUNK)) & (ci <= ri)).astype(_BF16)
    c_k, c_v, c_g = GLA_DK, 2 * GLA_DK, 2 * GLA_DK + GLA_DV
    chunk = lambda c: slice(c * CHUNK, (c + 1) * CHUNK)

    a_hi, a_lo = _split2(a_ref[...])
    w_hi, w_lo = _split2(aup_ref[...])
    z = ab_ref[...] + _dot(a_hi, w_hi) + _dot(a_lo, w_hi) + _dot(a_hi, w_lo)
    log_a = (jnp.minimum(z, 0.0) - jnp.log1p(jnp.exp(-jnp.abs(z)))) * (1.0 / GLA_GATE_TAU)
    la_hi, la_lo = _split2(log_a)
    yield

    state = jnp.where(first_tile, 0.0, st_ref[...])
    for s in range(rows // GLA_SUB):
        r = slice(s * GLA_SUB, (s + 1) * GLA_SUB)
        cum = _dot(lower, la_hi[r]) + _dot(lower, la_lo[r])
        totals = [cum[(c + 1) * CHUNK - 1:(c + 1) * CHUNK, :] for c in range(nchunk)]
        before = [jnp.zeros_like(totals[0])]
        for c in range(nchunk):
            before.append(before[c] + totals[c])
        total_b = jnp.concatenate([jnp.broadcast_to(tr, (CHUNK, GLA_DK)) for tr in totals], axis=0)
        k = p_ref[r, c_k:c_v]
        q_dec = p_ref[r, 0:c_k] * (GLA_DK ** -0.5) * jnp.exp(cum)
        k_inv = (k * jnp.exp(-cum)).astype(_BF16)
        k_dec = k * jnp.exp(total_b - cum)
        vb = p_ref[r, c_v:c_g].astype(_BF16)
        yield

        score_rows = []
        for c in range(nchunk):
            keys = [(k_dec[chunk(c2)] * jnp.exp(before[c] - before[c2 + 1])).astype(_BF16)
                    for c2 in range(c)]
            keys.append(k_inv[chunk(c)])
            if c + 1 < nchunk:
                keys.append(jnp.zeros(((nchunk - 1 - c) * CHUNK, GLA_DK), _BF16))
            score_rows.append(_dot_nt(q_dec[chunk(c)].astype(_BF16), jnp.concatenate(keys, axis=0)))
        scores = jnp.where(ci <= ri, jnp.concatenate(score_rows, axis=0), 0.0).astype(_BF16)
        yield

        q_state = jnp.concatenate([q_dec[chunk(c)] * jnp.exp(before[c]) for c in range(nchunk)], axis=0)
        o = _dot(scores, vb) + _dot_nt(q_state.astype(_BF16), state.astype(_BF16))
        k_end = jnp.concatenate([k_dec[chunk(c)] * jnp.exp(before[nchunk] - before[c + 1])
                                 for c in range(nchunk)], axis=0)
        yield
        state = state * jnp.exp(before[nchunk]) + _dot_tn(vb, k_end.astype(_BF16))
        yield

        o = o * lax.rsqrt(jnp.mean(o * o, axis=-1, keepdims=True) + NORM_EPS)
        g = p_ref[r, c_g:]
        o_ref[r, :] = (o * ng_ref[...] * (g / (1.0 + jnp.exp(-g)))).astype(o_ref.dtype)
        yield

    st_ref[...] = state


def _gla_kernel(xg_ref, ssq_ref, a_ref, wq_ref, wk_ref, wv_ref, wg_ref, aup_ref, ab_ref, ng_ref,
                o_ref, pa_ref, pb_ref, st_ref, *, tiles_per_seq):
    n = pl.program_id(1)
    w_refs = (wq_ref, wk_ref, wv_ref, wg_ref)

    @pl.when(n == 0)
    def _():
        pa_ref[...] = jnp.zeros_like(pa_ref)
        pb_ref[...] = jnp.zeros_like(pb_ref)
        st_ref[...] = jnp.zeros_like(st_ref)

    first_tile = (jnp.maximum(n - 1, 0) % tiles_per_seq) == 0

    def step(p_new, p_old):
        pieces = _gla_project_pieces(xg_ref, ssq_ref, w_refs, p_new)
        for _ in _gla_math(p_old, a_ref, aup_ref, ab_ref, ng_ref, o_ref, st_ref, first_tile):
            if pieces:
                pieces.pop(0)()
        for piece in pieces:
            piece()

    @pl.when(n % 2 == 0)
    def _():
        step(pa_ref, pb_ref)

    @pl.when(n % 2 == 1)
    def _():
        step(pb_ref, pa_ref)


def _gla_mixer(xg, ssq, a_low, w_in, a_up, a_bias, norm_g, layer, seq, tr=512):
    m, d = xg.shape
    nt = m // tr
    kern = functools.partial(_gla_kernel, tiles_per_seq=seq // tr)
    cur = lambda h, n: (jnp.minimum(n, nt - 1), 0)
    prev = lambda h, n: (jnp.maximum(n - 1, 0), 0)
    wspec = lambda off, width: pl.BlockSpec((None, d, width), lambda h, n: (layer, 0, off // width + h))
    return pl.pallas_call(
        kern,
        grid=(GLA_HEADS, nt + 1),
        in_specs=[pl.BlockSpec((tr, d), cur),
                  pl.BlockSpec((tr, LANES), cur),
                  pl.BlockSpec((tr, LANES), prev),
                  wspec(OFF_Q, GLA_DK), wspec(OFF_K, GLA_DK), wspec(OFF_V, GLA_DV), wspec(OFF_G, GLA_DV),
                  pl.BlockSpec((None, LANES, GLA_DK), lambda h, n: (layer, 0, h)),
                  pl.BlockSpec((None, 1, GLA_DK), lambda h, n: (layer, 0, h)),
                  pl.BlockSpec((None, 1, GLA_DV), lambda h, n: (layer, 0, h))],
        out_specs=pl.BlockSpec((tr, GLA_DV), lambda h, n: (jnp.maximum(n - 1, 0), h)),
        out_shape=jax.ShapeDtypeStruct((m, GLA_WIDTH), _BF16),
        scratch_shapes=[pltpu.VMEM((tr, GLA_COLS), _F32),
                        pltpu.VMEM((tr, GLA_COLS), _F32),
                        pltpu.VMEM((GLA_DV, GLA_DK), _F32)],
        compiler_params=_params("parallel", "arbitrary"),
        name="gla_mixer",
    )(xg, ssq, a_low, w_in, w_in, w_in, w_in, a_up, a_bias, norm_g)


def _init_ssq(ssq_ref):
    @pl.when(pl.program_id(1) == 0)
    def _():
        ssq_ref[...] = jnp.zeros_like(ssq_ref)


def _emit_stream(xn, gn_ref, o_ref, xg_ref, ssq_ref):
    o_ref[...] = xn
    xg_ref[...] = (xn * gn_ref[...]).astype(xg_ref.dtype)
    ssq_ref[...] += _lane_tile_sum(xn * xn)


def _out_kernel(yc_ref, yg_ref, w1_ref, w2_ref, x_ref, gn_ref, o_ref, xg_ref, ssq_ref):
    _init_ssq(ssq_ref)
    xn = x_ref[...] + _dot(yc_ref[...], w1_ref[...]) + _dot(yg_ref[...], w2_ref[...])
    _emit_stream(xn, gn_ref, o_ref, xg_ref, ssq_ref)


def _stream_out(m, d, tm, tn):
    specs = [pl.BlockSpec((tm, tn), lambda i, j: (i, j)),
             pl.BlockSpec((tm, tn), lambda i, j: (i, j)),
             pl.BlockSpec((tm, LANES), lambda i, j: (i, 0))]
    shapes = [jax.ShapeDtypeStruct((m, d), _F32),
              jax.ShapeDtypeStruct((m, d), _BF16),
              jax.ShapeDtypeStruct((m, LANES), _F32)]
    return specs, shapes


def _out_proj(y_conv, y_gla, w_out, x, g_next, layer, tm=1024, tn=512):
    m, d = x.shape
    kc = y_conv.shape[1]
    kg = y_gla.shape[1]
    out_specs, out_shape = _stream_out(m, d, tm, tn)
    return pl.pallas_call(
        _out_kernel,
        grid=(m // tm, d // tn),
        in_specs=[pl.BlockSpec((tm, kc), lambda i, j: (i, 0)),
                  pl.BlockSpec((tm, kg), lambda i, j: (i, 0)),
                  pl.BlockSpec((None, kc, tn), lambda i, j: (layer, 0, j)),
                  pl.BlockSpec((None, kg, tn), lambda i, j: (layer, 1, j)),
                  pl.BlockSpec((tm, tn), lambda i, j: (i, j)),
                  pl.BlockSpec((1, tn), lambda i, j: (0, j))],
        out_specs=out_specs,
        out_shape=out_shape,
        compiler_params=_params("parallel", "arbitrary"),
        name="out_proj",
    )(y_conv, y_gla, w_out, w_out, x, g_next.reshape(1, d))


def _fetch_row_tile(h_hbm, hbuf, sem):
    i, j = pl.program_id(0), pl.program_id(1)
    tm = hbuf.shape[1]
    slot = i % 2

    def tile_copy(tile, s):
        rows = pl.ds(pl.multiple_of(tile * tm, tm), tm)
        return pltpu.make_async_copy(h_hbm.at[rows, :], hbuf.at[s], sem.at[s])

    @pl.when(j == 0)
    def _():
        @pl.when(i == 0)
        def _():
            tile_copy(0, 0).start()

        tile_copy(i, slot).wait()

        @pl.when(i + 1 < pl.num_programs(0))
        def _():
            tile_copy(i + 1, 1 - slot).start()

    return slot


def _down_kernel(h_hbm, w_ref, x_ref, gn_ref, o_ref, xg_ref, ssq_ref, hbuf, sem):
    slot = _fetch_row_tile(h_hbm, hbuf, sem)
    _init_ssq(ssq_ref)
    xn = x_ref[...] + _dot(hbuf[slot], w_ref[...])
    _emit_stream(xn, gn_ref, o_ref, xg_ref, ssq_ref)


def _down_last_kernel(h_hbm, w_ref, x_ref, o_ref, hbuf, sem):
    slot = _fetch_row_tile(h_hbm, hbuf, sem)
    o_ref[...] = x_ref[...] + _dot(hbuf[slot], w_ref[...])


def _down_proj(h, w_down, x, g_next, layer, tm=512, tn=512):
    m, d = x.shape
    f = h.shape[1]
    in_specs = [pl.BlockSpec(memory_space=pl.ANY),
                pl.BlockSpec((None, f, tn), lambda i, j: (layer, 0, j)),
                pl.BlockSpec((tm, tn), lambda i, j: (i, j))]
    scratch = [pltpu.VMEM((2, tm, f), h.dtype), pltpu.SemaphoreType.DMA((2,))]
    params = _params("arbitrary", "arbitrary")
    if g_next is None:
        return pl.pallas_call(
            _down_last_kernel,
            grid=(m // tm, d // tn),
            in_specs=in_specs,
            out_specs=pl.BlockSpec((tm, tn), lambda i, j: (i, j)),
            out_shape=jax.ShapeDtypeStruct((m, d), _F32),
            scratch_shapes=scratch,
            compiler_params=params,
            name="ffn_down_last",
        )(h, w_down, x)
    out_specs, out_shape = _stream_out(m, d, tm, tn)
    return pl.pallas_call(
        _down_kernel,
        grid=(m // tm, d // tn),
        in_specs=in_specs + [pl.BlockSpec((1, tn), lambda i, j: (0, j))],
        out_specs=out_specs,
        out_shape=out_shape,
        scratch_shapes=scratch,
        compiler_params=params,
        name="ffn_down",
    )(h, w_down, x, g_next.reshape(1, d))


def _gate_up_kernel(xg_ref, ssq_ref, wg_ref, wu_ref, o_ref, *, sub):
    wg = wg_ref[...].astype(_BF16)
    wu = wu_ref[...].astype(_BF16)
    rs_all = _row_scale(ssq_ref, xg_ref.shape[1])
    for r in range(xg_ref.shape[0] // sub):
        sl = slice(r * sub, (r + 1) * sub)
        xg = xg_ref[sl, :]
        rs = rs_all[sl]
        g = _dot(xg, wg) * rs
        u = _dot(xg, wu) * rs
        o_ref[sl, :] = ((g / (1.0 + jnp.exp(-g))) * u).astype(o_ref.dtype)


def _gate_up(xg, ssq, w_gate, w_up, layer, tm=2048, tf=256, sub=512):
    m, d = xg.shape
    f = w_gate.shape[2]
    return pl.pallas_call(
        functools.partial(_gate_up_kernel, sub=sub),
        grid=(m // tm, f // tf),
        in_specs=[pl.BlockSpec((tm, d), lambda i, j: (i, 0)),
                  pl.BlockSpec((tm, LANES), lambda i, j: (i, 0)),
                  pl.BlockSpec((None, d, tf), lambda i, j: (layer, 0, j)),
                  pl.BlockSpec((None, d, tf), lambda i, j: (layer, 0, j))],
        out_specs=pl.BlockSpec((tm, tf), lambda i, j: (i, j)),
        out_shape=jax.ShapeDtypeStruct((m, f), _BF16),
        compiler_params=_params("parallel", "parallel", vmem=VMEM_LIMIT_GATE_UP),
        name="ffn_gate_up",
    )(xg, ssq, w_gate, w_up)


def kernel(x, mix_norm, w_in, conv_w, gla_a_up, gla_a_bias, gla_norm, w_out,
           ffn_norm, w_gate, w_up, w_down, final_norm):
    batch, seq, d = x.shape
    depth = w_in.shape[0]
    xf = x.reshape(batch * seq, d)

    w_in_b = w_in.astype(_BF16)
    w_a = jnp.pad(w_in[:, :, OFF_A:], ((0, 0), (0, 0), (0, LANES - GLA_GATE_RANK))).astype(_BF16)
    w_out_b = w_out.astype(_BF16)
    w_down_b = w_down.astype(_BF16)
    a_up = jnp.pad(gla_a_up, ((0, 0), (0, LANES - GLA_GATE_RANK), (0, 0)))
    a_bias = gla_a_bias.reshape(depth, 1, -1)
    norm_g = gla_norm.reshape(depth, 1, -1)

    xg, ssq = _prep(xf, mix_norm[0])
    for l in range(depth):
        y_conv = _conv_mixer(xg, ssq, w_in_b, conv_w, l, seq)
        a_low = _gate_low(xg, ssq, w_a, l)
        y_gla = _gla_mixer(xg, ssq, a_low, w_in_b, a_up, a_bias, norm_g, l, seq)
        xf, xg, ssq = _out_proj(y_conv, y_gla, w_out_b, xf, ffn_norm[l], l)
        h = _gate_up(xg, ssq, w_gate, w_up, l)
        if l + 1 < depth:
            xf, xg, ssq = _down_proj(h, w_down_b, xf, mix_norm[l + 1], l)
        else:
            xf = _down_proj(h, w_down_b, xf, None, l)

    return _rmsnorm(xf, final_norm, x.dtype).reshape(batch, seq, d)
```

```python
import functools

import jax
import jax.numpy as jnp
from jax import lax
from jax.experimental import pallas as pl
from jax.experimental.pallas import tpu as pltpu

CHUNK = 64
CONV_WIDTH = 2048
CONV_K = 3
GLA_WIDTH = 2048
GLA_HEADS = 4
GLA_DK = 256
GLA_DV = 512
GLA_GATE_RANK = 16
GLA_GATE_TAU = 16.0
NORM_EPS = 1e-6

OFF_Q = 3 * CONV_WIDTH
OFF_K = OFF_Q + GLA_HEADS * GLA_DK
OFF_V = OFF_K + GLA_HEADS * GLA_DK
OFF_G = OFF_V + GLA_WIDTH
OFF_A = OFF_G + GLA_WIDTH

LANES = 128
SUBLANES = 8
GLA_SUB = 256
GLA_COLS = 2 * GLA_DK + 2 * GLA_DV
VMEM_LIMIT = 56 * 1024 * 1024
VMEM_LIMIT_GATE_UP = 60 * 1024 * 1024

_BF16 = jnp.bfloat16
_F32 = jnp.float32


def _params(*sem, vmem=VMEM_LIMIT):
    return pltpu.CompilerParams(dimension_semantics=sem, vmem_limit_bytes=vmem)


def _dot(a, b):
    return jnp.dot(a, b, preferred_element_type=_F32)


def _dot_nt(a, b):
    return lax.dot_general(a, b, (((1,), (1,)), ((), ())), preferred_element_type=_F32)


def _dot_tn(a, b):
    return lax.dot_general(a, b, (((0,), (0,)), ((), ())), preferred_element_type=_F32)


def _lane_tile_sum(v):
    acc = v[:, 0:LANES]
    for c in range(1, v.shape[1] // LANES):
        acc = acc + v[:, c * LANES:(c + 1) * LANES]
    return acc


def _row_scale(ssq_ref, d):
    return lax.rsqrt(jnp.sum(ssq_ref[...], axis=-1, keepdims=True) * (1.0 / d) + NORM_EPS)


def _split2(x):
    hi = x.astype(_BF16)
    return hi, (x - hi.astype(_F32)).astype(_BF16)


def _prep_kernel(x_ref, g_ref, xg_ref, ssq_ref):
    x = x_ref[...]
    xg_ref[...] = (x * g_ref[...]).astype(xg_ref.dtype)
    ssq_ref[...] = _lane_tile_sum(x * x)


def _prep(x, g, tm=256):
    m, d = x.shape
    return pl.pallas_call(
        _prep_kernel,
        grid=(m // tm,),
        in_specs=[pl.BlockSpec((tm, d), lambda i: (i, 0)),
                  pl.BlockSpec((1, d), lambda i: (0, 0))],
        out_specs=[pl.BlockSpec((tm, d), lambda i: (i, 0)),
                   pl.BlockSpec((tm, LANES), lambda i: (i, 0))],
        out_shape=[jax.ShapeDtypeStruct((m, d), _BF16),
                   jax.ShapeDtypeStruct((m, LANES), _F32)],
        compiler_params=_params("parallel"),
        name="prep",
    )(x, g.reshape(1, d))


def _rmsnorm_kernel(x_ref, g_ref, o_ref):
    x = x_ref[...]
    ms = jnp.mean(x * x, axis=-1, keepdims=True)
    o_ref[...] = (x * lax.rsqrt(ms + NORM_EPS) * g_ref[...]).astype(o_ref.dtype)


def _rmsnorm(x, g, out_dtype, tm=256):
    m, d = x.shape
    return pl.pallas_call(
        _rmsnorm_kernel,
        grid=(m // tm,),
        in_specs=[pl.BlockSpec((tm, d), lambda i: (i, 0)),
                  pl.BlockSpec((1, d), lambda i: (0, 0))],
        out_specs=pl.BlockSpec((tm, d), lambda i: (i, 0)),
        out_shape=jax.ShapeDtypeStruct((m, d), out_dtype),
        compiler_params=_params("parallel"),
        name="rmsnorm",
    )(x, g.reshape(1, d))


def _conv_kernel(xg_ref, ssq_ref, wb_ref, wc_ref, wh_ref, cw_ref, o_ref, carry_ref, *, tiles_per_seq):
    i = pl.program_id(1)

    @pl.when(i % tiles_per_seq == 0)
    def _():
        carry_ref[...] = jnp.zeros_like(carry_ref)

    tm, tn = o_ref.shape
    xg = xg_ref[...]
    rs = _row_scale(ssq_ref, xg_ref.shape[1])
    u = (_dot(xg, wc_ref[...]) * rs) * (_dot(xg, wh_ref[...]) * rs)
    row = lax.broadcasted_iota(jnp.int32, u.shape, 0)
    prev = carry_ref[...]
    p1 = prev[SUBLANES - 1:SUBLANES, :]
    p2 = prev[SUBLANES - 2:SUBLANES - 1, :]
    u1 = jnp.where(row == 0, p1, pltpu.roll(u, 1, axis=0))
    u2 = jnp.where(row == 0, p2, jnp.where(row == 1, p1, pltpu.roll(u, 2, axis=0)))
    carry_ref[...] = u[tm - SUBLANES:, :]
    cw = cw_ref[...]
    y = cw[0:1, :] * u2 + cw[1:2, :] * u1 + cw[2:3, :] * u
    o_ref[...] = (_dot(xg, wb_ref[...]) * rs * y).astype(o_ref.dtype)


def _conv_mixer(xg, ssq, w_in, conv_w, layer, seq, tm=1024, tn=512):
    m, d = xg.shape
    nb = CONV_WIDTH // tn
    kern = functools.partial(_conv_kernel, tiles_per_seq=seq // tm)
    wspec = lambda group: pl.BlockSpec((None, d, tn), lambda j, i: (layer, 0, group * nb + j))
    return pl.pallas_call(
        kern,
        grid=(nb, m // tm),
        in_specs=[pl.BlockSpec((tm, d), lambda j, i: (i, 0)),
                  pl.BlockSpec((tm, LANES), lambda j, i: (i, 0)),
                  wspec(0), wspec(1), wspec(2),
                  pl.BlockSpec((None, CONV_K, tn), lambda j, i: (layer, 0, j))],
        out_specs=pl.BlockSpec((tm, tn), lambda j, i: (i, j)),
        out_shape=jax.ShapeDtypeStruct((m, CONV_WIDTH), _BF16),
        scratch_shapes=[pltpu.VMEM((SUBLANES, tn), _F32)],
        compiler_params=_params("parallel", "arbitrary"),
        name="conv_mixer",
    )(xg, ssq, w_in, w_in, w_in, conv_w)


def _gate_low_kernel(xg_ref, ssq_ref, wa_ref, o_ref):
    o_ref[...] = _dot(xg_ref[...], wa_ref[...]) * _row_scale(ssq_ref, xg_ref.shape[1])


def _gate_low(xg, ssq, w_a, layer, tm=1024):
    m, d = xg.shape
    return pl.pallas_call(
        _gate_low_kernel,
        grid=(m // tm,),
        in_specs=[pl.BlockSpec((tm, d), lambda i: (i, 0)),
                  pl.BlockSpec((tm, LANES), lambda i: (i, 0)),
                  pl.BlockSpec((None, d, LANES), lambda i: (layer, 0, 0))],
        out_specs=pl.BlockSpec((tm, LANES), lambda i: (i, 0)),
        out_shape=jax.ShapeDtypeStruct((m, LANES), _F32),
        compiler_params=_params("parallel"),
        name="gate_low",
    )(xg, ssq, w_a)


def _gla_project_pieces(xg_ref, ssq_ref, w_refs, p_ref):
    d = xg_ref.shape[1]
    pieces = []
    for half in range(xg_ref.shape[0] // GLA_SUB):
        r = slice(half * GLA_SUB, (half + 1) * GLA_SUB)
        col = 0
        for w_ref in w_refs:
            width = w_ref.shape[1]

            def piece(r=r, col=col, width=width, w_ref=w_ref):
                rs = _row_scale(ssq_ref[r, :], d)
                p_ref[r, col:col + width] = _dot(xg_ref[r, :], w_ref[...]) * rs
            pieces.append(piece)
            col += width
    return pieces


def _gla_math(p_ref, a_ref, aup_ref, ab_ref, ng_ref, o_ref, st_ref, first_tile):
    rows = p_ref.shape[0]
    nchunk = GLA_SUB // CHUNK
    ri = lax.broadcasted_iota(jnp.int32, (GLA_SUB, GLA_SUB), 0)
    ci = lax.broadcasted_iota(jnp.int32, (GLA_SUB, GLA_SUB), 1)
    lower = (((ri // CHUNK) == (ci // CHUNK)) & (ci <= ri)).astype(_BF16)
    c_k, c_v, c_g = GLA_DK, 2 * GLA_DK, 2 * GLA_DK + GLA_DV
    chunk = lambda c: slice(c * CHUNK, (c + 1) * CHUNK)

    a_hi, a_lo = _split2(a_ref[...])
    w_hi, w_lo = _split2(aup_ref[...])
    z = ab_ref[...] + _dot(a_hi, w_hi) + _dot(a_lo, w_hi) + _dot(a_hi, w_lo)
    log_a = (jnp.minimum(z, 0.0) - jnp.log1p(jnp.exp(-jnp.abs(z)))) * (1.0 / GLA_GATE_TAU)
    la_hi, la_lo = _split2(log_a)
    yield

    state = jnp.where(first_tile, 0.0, st_ref[...])
    for s in range(rows // GLA_SUB):
        r = slice(s * GLA_SUB, (s + 1) * GLA_SUB)
        cum = _dot(lower, la_hi[r]) + _dot(lower, la_lo[r])
        totals = [cum[(c + 1) * CHUNK - 1:(c + 1) * CHUNK, :] for c in range(nchunk)]
        before = [jnp.zeros_like(totals[0])]
        for c in range(nchunk):
            before.append(before[c] + totals[c])
        total_b = jnp.concatenate([jnp.broadcast_to(tr, (CHUNK, GLA_DK)) for tr in totals], axis=0)
        k = p_ref[r, c_k:c_v]
        q_dec = p_ref[r, 0:c_k] * (GLA_DK ** -0.5) * jnp.exp(cum)
        k_inv = (k * jnp.exp(-cum)).astype(_BF16)
        k_dec = k * jnp.exp(total_b - cum)
        vb = p_ref[r, c_v:c_g].astype(_BF16)
        yield

        score_rows = []
        for c in range(nchunk):
            keys = [(k_dec[chunk(c2)] * jnp.exp(before[c] - before[c2 + 1])).astype(_BF16)
                    for c2 in range(c)]
            keys.append(k_inv[chunk(c)])
            if c + 1 < nchunk:
                keys.append(jnp.zeros(((nchunk - 1 - c) * CHUNK, GLA_DK), _BF16))
            score_rows.append(_dot_nt(q_dec[chunk(c)].astype(_BF16), jnp.concatenate(keys, axis=0)))
        scores = jnp.where(ci <= ri, jnp.concatenate(score_rows, axis=0), 0.0).astype(_BF16)
        yield

        q_state = jnp.concatenate([q_dec[chunk(c)] * jnp.exp(before[c]) for c in range(nchunk)], axis=0)
        o = _dot(scores, vb) + _dot_nt(q_state.astype(_BF16), state.astype(_BF16))
        k_end = jnp.concatenate([k_dec[chunk(c)] * jnp.exp(before[nchunk] - before[c + 1])
                                 for c in range(nchunk)], axis=0)
        yield
        state = state * jnp.exp(before[nchunk]) + _dot_tn(vb, k_end.astype(_BF16))
        yield

        o = o * lax.rsqrt(jnp.mean(o * o, axis=-1, keepdims=True) + NORM_EPS)
        g = p_ref[r, c_g:]
        o_ref[r, :] = (o * ng_ref[...] * (g / (1.0 + jnp.exp(-g)))).astype(o_ref.dtype)
        yield

    st_ref[...] = state


def _gla_kernel(xg_ref, ssq_ref, a_ref, wq_ref, wk_ref, wv_ref, wg_ref, aup_ref, ab_ref, ng_ref,
                o_ref, pa_ref, pb_ref, st_ref, *, tiles_per_seq):
    n = pl.program_id(1)
    w_refs = (wq_ref, wk_ref, wv_ref, wg_ref)

    @pl.when(n == 0)
    def _():
        pa_ref[...] = jnp.zeros_like(pa_ref)
        pb_ref[...] = jnp.zeros_like(pb_ref)
        st_ref[...] = jnp.zeros_like(st_ref)

    first_tile = (jnp.maximum(n - 1, 0) % tiles_per_seq) == 0

    def step(p_new, p_old):
        pieces = _gla_project_pieces(xg_ref, ssq_ref, w_refs, p_new)
        for _ in _gla_math(p_old, a_ref, aup_ref, ab_ref, ng_ref, o_ref, st_ref, first_tile):
            if pieces:
                pieces.pop(0)()
        for piece in pieces:
            piece()

    @pl.when(n % 2 == 0)
    def _():
        step(pa_ref, pb_ref)

    @pl.when(n % 2 == 1)
    def _():
        step(pb_ref, pa_ref)


def _gla_mixer(xg, ssq, a_low, w_in, a_up, a_bias, norm_g, layer, seq, tr=512):
    m, d = xg.shape
    nt = m // tr
    kern = functools.partial(_gla_kernel, tiles_per_seq=seq // tr)
    cur = lambda h, n: (jnp.minimum(n, nt - 1), 0)
    prev = lambda h, n: (jnp.maximum(n - 1, 0), 0)
    wspec = lambda off, width: pl.BlockSpec((None, d, width), lambda h, n: (layer, 0, off // width + h))
    return pl.pallas_call(
        kern,
        grid=(GLA_HEADS, nt + 1),
        in_specs=[pl.BlockSpec((tr, d), cur),
                  pl.BlockSpec((tr, LANES), cur),
                  pl.BlockSpec((tr, LANES), prev),
                  wspec(OFF_Q, GLA_DK), wspec(OFF_K, GLA_DK), wspec(OFF_V, GLA_DV), wspec(OFF_G, GLA_DV),
                  pl.BlockSpec((None, LANES, GLA_DK), lambda h, n: (layer, 0, h)),
                  pl.BlockSpec((None, 1, GLA_DK), lambda h, n: (layer, 0, h)),
                  pl.BlockSpec((None, 1, GLA_DV), lambda h, n: (layer, 0, h))],
        out_specs=pl.BlockSpec((tr, GLA_DV), lambda h, n: (jnp.maximum(n - 1, 0), h)),
        out_shape=jax.ShapeDtypeStruct((m, GLA_WIDTH), _BF16),
        scratch_shapes=[pltpu.VMEM((tr, GLA_COLS), _F32),
                        pltpu.VMEM((tr, GLA_COLS), _F32),
                        pltpu.VMEM((GLA_DV, GLA_DK), _F32)],
        compiler_params=_params("parallel", "arbitrary"),
        name="gla_mixer",
    )(xg, ssq, a_low, w_in, w_in, w_in, w_in, a_up, a_bias, norm_g)


def _init_ssq(ssq_ref):
    @pl.when(pl.program_id(1) == 0)
    def _():
        ssq_ref[...] = jnp.zeros_like(ssq_ref)


def _emit_stream(xn, gn_ref, o_ref, xg_ref, ssq_ref):
    o_ref[...] = xn
    xg_ref[...] = (xn * gn_ref[...]).astype(xg_ref.dtype)
    ssq_ref[...] += _lane_tile_sum(xn * xn)


def _out_kernel(yc_hbm, yg_hbm, w1_ref, w2_ref, x_ref, gn_ref, o_ref, xg_ref, ssq_ref,
                ycbuf, ygbuf, sem_c, sem_g):
    slot = _fetch_row_tile(yc_hbm, ycbuf, sem_c)
    _fetch_row_tile(yg_hbm, ygbuf, sem_g)
    _init_ssq(ssq_ref)
    xn = x_ref[...] + _dot(ycbuf[slot], w1_ref[...]) + _dot(ygbuf[slot], w2_ref[...])
    _emit_stream(xn, gn_ref, o_ref, xg_ref, ssq_ref)


def _stream_out(m, d, tm, tn):
    specs = [pl.BlockSpec((tm, tn), lambda i, j: (i, j)),
             pl.BlockSpec((tm, tn), lambda i, j: (i, j)),
             pl.BlockSpec((tm, LANES), lambda i, j: (i, 0))]
    shapes = [jax.ShapeDtypeStruct((m, d), _F32),
              jax.ShapeDtypeStruct((m, d), _BF16),
              jax.ShapeDtypeStruct((m, LANES), _F32)]
    return specs, shapes


def _out_proj(y_conv, y_gla, w_out, x, g_next, layer, tm=1024, tn=512):
    m, d = x.shape
    kc = y_conv.shape[1]
    kg = y_gla.shape[1]
    out_specs, out_shape = _stream_out(m, d, tm, tn)
    return pl.pallas_call(
        _out_kernel,
        grid=(m // tm, d // tn),
        in_specs=[pl.BlockSpec(memory_space=pl.ANY),
                  pl.BlockSpec(memory_space=pl.ANY),
                  pl.BlockSpec((None, kc, tn), lambda i, j: (layer, 0, j)),
                  pl.BlockSpec((None, kg, tn), lambda i, j: (layer, 1, j)),
                  pl.BlockSpec((tm, tn), lambda i, j: (i, j)),
                  pl.BlockSpec((1, tn), lambda i, j: (0, j))],
        out_specs=out_specs,
        out_shape=out_shape,
        scratch_shapes=[pltpu.VMEM((2, tm, kc), y_conv.dtype), pltpu.VMEM((2, tm, kg), y_gla.dtype),
                        pltpu.SemaphoreType.DMA((2,)), pltpu.SemaphoreType.DMA((2,))],
        compiler_params=_params("arbitrary", "arbitrary"),
        name="out_proj",
    )(y_conv, y_gla, w_out, w_out, x, g_next.reshape(1, d))


def _fetch_row_tile(h_hbm, hbuf, sem):
    i, j = pl.program_id(0), pl.program_id(1)
    tm = hbuf.shape[1]
    slot = i % 2

    def tile_copy(tile, s):
        rows = pl.ds(pl.multiple_of(tile * tm, tm), tm)
        return pltpu.make_async_copy(h_hbm.at[rows, :], hbuf.at[s], sem.at[s])

    @pl.when(j == 0)
    def _():
        @pl.when(i == 0)
        def _():
            tile_copy(0, 0).start()

        tile_copy(i, slot).wait()

        @pl.when(i + 1 < pl.num_programs(0))
        def _():
            tile_copy(i + 1, 1 - slot).start()

    return slot


def _down_kernel(h_hbm, w_ref, x_ref, gn_ref, o_ref, xg_ref, ssq_ref, hbuf, sem):
    slot = _fetch_row_tile(h_hbm, hbuf, sem)
    _init_ssq(ssq_ref)
    xn = x_ref[...] + _dot(hbuf[slot], w_ref[...])
    _emit_stream(xn, gn_ref, o_ref, xg_ref, ssq_ref)


def _down_last_kernel(h_hbm, w_ref, x_ref, o_ref, hbuf, sem):
    slot = _fetch_row_tile(h_hbm, hbuf, sem)
    o_ref[...] = x_ref[...] + _dot(hbuf[slot], w_ref[...])


def _down_proj(h, w_down, x, g_next, layer, tm=512, tn=512):
    m, d = x.shape
    f = h.shape[1]
    in_specs = [pl.BlockSpec(memory_space=pl.ANY),
                pl.BlockSpec((None, f, tn), lambda i, j: (layer, 0, j)),
                pl.BlockSpec((tm, tn), lambda i, j: (i, j))]
    scratch = [pltpu.VMEM((2, tm, f), h.dtype), pltpu.SemaphoreType.DMA((2,))]
    params = _params("arbitrary", "arbitrary")
    if g_next is None:
        return pl.pallas_call(
            _down_last_kernel,
            grid=(m // tm, d // tn),
            in_specs=in_specs,
            out_specs=pl.BlockSpec((tm, tn), lambda i, j: (i, j)),
            out_shape=jax.ShapeDtypeStruct((m, d), _F32),
            scratch_shapes=scratch,
            compiler_params=params,
            name="ffn_down_last",
        )(h, w_down, x)
    out_specs, out_shape = _stream_out(m, d, tm, tn)
    return pl.pallas_call(
        _down_kernel,
        grid=(m // tm, d // tn),
        in_specs=in_specs + [pl.BlockSpec((1, tn), lambda i, j: (0, j))],
        out_specs=out_specs,
        out_shape=out_shape,
        scratch_shapes=scratch,
        compiler_params=params,
        name="ffn_down",
    )(h, w_down, x, g_next.reshape(1, d))


def _gate_up_kernel(xg_hbm, ssq_ref, wg_ref, wu_ref, o_ref, xbuf, sem, *, sub):
    slot = _fetch_row_tile(xg_hbm, xbuf, sem)
    wg = wg_ref[...].astype(_BF16)
    wu = wu_ref[...].astype(_BF16)
    rs_all = _row_scale(ssq_ref, xbuf.shape[2])
    for r in range(xbuf.shape[1] // sub):
        sl = slice(r * sub, (r + 1) * sub)
        xg = xbuf[slot, sl, :]
        rs = rs_all[sl]
        g = _dot(xg, wg) * rs
        u = _dot(xg, wu) * rs
        o_ref[sl, :] = ((g / (1.0 + jnp.exp(-g))) * u).astype(o_ref.dtype)


def _gate_up(xg, ssq, w_gate, w_up, layer, tm=2048, tf=256, sub=512):
    m, d = xg.shape
    f = w_gate.shape[2]
    return pl.pallas_call(
        functools.partial(_gate_up_kernel, sub=sub),
        grid=(m // tm, f // tf),
        in_specs=[pl.BlockSpec(memory_space=pl.ANY),
                  pl.BlockSpec((tm, LANES), lambda i, j: (i, 0)),
                  pl.BlockSpec((None, d, tf), lambda i, j: (layer, 0, j)),
                  pl.BlockSpec((None, d, tf), lambda i, j: (layer, 0, j))],
        out_specs=pl.BlockSpec((tm, tf), lambda i, j: (i, j)),
        out_shape=jax.ShapeDtypeStruct((m, f), _BF16),
        scratch_shapes=[pltpu.VMEM((2, tm, d), xg.dtype), pltpu.SemaphoreType.DMA((2,))],
        compiler_params=_params("arbitrary", "arbitrary", vmem=VMEM_LIMIT_GATE_UP),
        name="ffn_gate_up",
    )(xg, ssq, w_gate, w_up)


def kernel(x, mix_norm, w_in, conv_w, gla_a_up, gla_a_bias, gla_norm, w_out,
           ffn_norm, w_gate, w_up, w_down, final_norm):
    batch, seq, d = x.shape
    depth = w_in.shape[0]
    xf = x.reshape(batch * seq, d)

    w_in_b = w_in.astype(_BF16)
    w_a = jnp.pad(w_in[:, :, OFF_A:], ((0, 0), (0, 0), (0, LANES - GLA_GATE_RANK))).astype(_BF16)
    w_out_b = w_out.astype(_BF16)
    w_down_b = w_down.astype(_BF16)
    a_up = jnp.pad(gla_a_up, ((0, 0), (0, LANES - GLA_GATE_RANK), (0, 0)))
    a_bias = gla_a_bias.reshape(depth, 1, -1)
    norm_g = gla_norm.reshape(depth, 1, -1)

    xg, ssq = _prep(xf, mix_norm[0])
    for l in range(depth):
        y_conv = _conv_mixer(xg, ssq, w_in_b, conv_w, l, seq)
        a_low = _gate_low(xg, ssq, w_a, l)
        y_gla = _gla_mixer(xg, ssq, a_low, w_in_b, a_up, a_bias, norm_g, l, seq)
        xf, xg, ssq = _out_proj(y_conv, y_gla, w_out_b, xf, ffn_norm[l], l)
        h = _gate_up(xg, ssq, w_gate, w_up, l)
        if l + 1 < depth:
            xf, xg, ssq = _down_proj(h, w_down_b, xf, mix_norm[l + 1], l)
        else:
            xf = _down_proj(h, w_down_b, xf, None, l)

    return _rmsnorm(xf, final_norm, x.dtype).reshape(batch, seq, d)
```

```python
import functools

import jax
import jax.numpy as jnp
from jax import lax
from jax.experimental import pallas as pl
from jax.experimental.pallas import tpu as pltpu

CHUNK = 64
CONV_WIDTH = 2048
CONV_K = 3
GLA_WIDTH = 2048
GLA_HEADS = 4
GLA_DK = 256
GLA_DV = 512
GLA_GATE_RANK = 16
GLA_GATE_TAU = 16.0
NORM_EPS = 1e-6

OFF_Q = 3 * CONV_WIDTH
OFF_K = OFF_Q + GLA_HEADS * GLA_DK
OFF_V = OFF_K + GLA_HEADS * GLA_DK
OFF_G = OFF_V + GLA_WIDTH
OFF_A = OFF_G + GLA_WIDTH

LANES = 128
SUBLANES = 8
GLA_SUB = 256
GLA_COLS = 2 * GLA_DK + 2 * GLA_DV
VMEM_LIMIT = 56 * 1024 * 1024
VMEM_LIMIT_GATE_UP = 60 * 1024 * 1024

_BF16 = jnp.bfloat16
_F32 = jnp.float32


def _params(*sem, vmem=VMEM_LIMIT):
    return pltpu.CompilerParams(dimension_semantics=sem, vmem_limit_bytes=vmem)


def _dot(a, b):
    return jnp.dot(a, b, preferred_element_type=_F32)


def _dot_nt(a, b):
    return lax.dot_general(a, b, (((1,), (1,)), ((), ())), preferred_element_type=_F32)


def _dot_tn(a, b):
    return lax.dot_general(a, b, (((0,), (0,)), ((), ())), preferred_element_type=_F32)


def _lane_tile_sum(v):
    acc = v[:, 0:LANES]
    for c in range(1, v.shape[1] // LANES):
        acc = acc + v[:, c * LANES:(c + 1) * LANES]
    return acc


def _row_scale(ssq_ref, d):
    return lax.rsqrt(jnp.sum(ssq_ref[...], axis=-1, keepdims=True) * (1.0 / d) + NORM_EPS)


def _split2(x):
    hi = x.astype(_BF16)
    return hi, (x - hi.astype(_F32)).astype(_BF16)


def _prep_kernel(x_ref, g_ref, xg_ref, ssq_ref):
    x = x_ref[...]
    xg_ref[...] = (x * g_ref[...]).astype(xg_ref.dtype)
    ssq_ref[...] = _lane_tile_sum(x * x)


def _prep(x, g, tm=256):
    m, d = x.shape
    return pl.pallas_call(
        _prep_kernel,
        grid=(m // tm,),
        in_specs=[pl.BlockSpec((tm, d), lambda i: (i, 0)),
                  pl.BlockSpec((1, d), lambda i: (0, 0))],
        out_specs=[pl.BlockSpec((tm, d), lambda i: (i, 0)),
                   pl.BlockSpec((tm, LANES), lambda i: (i, 0))],
        out_shape=[jax.ShapeDtypeStruct((m, d), _BF16),
                   jax.ShapeDtypeStruct((m, LANES), _F32)],
        compiler_params=_params("parallel"),
        name="prep",
    )(x, g.reshape(1, d))


def _rmsnorm_kernel(x_ref, g_ref, o_ref):
    x = x_ref[...]
    ms = jnp.mean(x * x, axis=-1, keepdims=True)
    o_ref[...] = (x * lax.rsqrt(ms + NORM_EPS) * g_ref[...]).astype(o_ref.dtype)


def _rmsnorm(x, g, out_dtype, tm=256):
    m, d = x.shape
    return pl.pallas_call(
        _rmsnorm_kernel,
        grid=(m // tm,),
        in_specs=[pl.BlockSpec((tm, d), lambda i: (i, 0)),
                  pl.BlockSpec((1, d), lambda i: (0, 0))],
        out_specs=pl.BlockSpec((tm, d), lambda i: (i, 0)),
        out_shape=jax.ShapeDtypeStruct((m, d), out_dtype),
        compiler_params=_params("parallel"),
        name="rmsnorm",
    )(x, g.reshape(1, d))


def _conv_kernel(xg_ref, ssq_ref, wb_ref, wc_ref, wh_ref, cw_ref, o_ref, carry_ref, *, tiles_per_seq):
    i = pl.program_id(1)

    @pl.when(i % tiles_per_seq == 0)
    def _():
        carry_ref[...] = jnp.zeros_like(carry_ref)

    tm, tn = o_ref.shape
    xg = xg_ref[...]
    rs = _row_scale(ssq_ref, xg_ref.shape[1])
    u = (_dot(xg, wc_ref[...]) * rs) * (_dot(xg, wh_ref[...]) * rs)
    row = lax.broadcasted_iota(jnp.int32, u.shape, 0)
    prev = carry_ref[...]
    p1 = prev[SUBLANES - 1:SUBLANES, :]
    p2 = prev[SUBLANES - 2:SUBLANES - 1, :]
    u1 = jnp.where(row == 0, p1, pltpu.roll(u, 1, axis=0))
    u2 = jnp.where(row == 0, p2, jnp.where(row == 1, p1, pltpu.roll(u, 2, axis=0)))
    carry_ref[...] = u[tm - SUBLANES:, :]
    cw = cw_ref[...]
    y = cw[0:1, :] * u2 + cw[1:2, :] * u1 + cw[2:3, :] * u
    o_ref[...] = (_dot(xg, wb_ref[...]) * rs * y).astype(o_ref.dtype)


def _conv_mixer(xg, ssq, w_in, conv_w, layer, seq, tm=1024, tn=512):
    m, d = xg.shape
    nb = CONV_WIDTH // tn
    kern = functools.partial(_conv_kernel, tiles_per_seq=seq // tm)
    wspec = lambda group: pl.BlockSpec((None, d, tn), lambda j, i: (layer, 0, group * nb + j))
    return pl.pallas_call(
        kern,
        grid=(nb, m // tm),
        in_specs=[pl.BlockSpec((tm, d), lambda j, i: (i, 0)),
                  pl.BlockSpec((tm, LANES), lambda j, i: (i, 0)),
                  wspec(0), wspec(1), wspec(2),
                  pl.BlockSpec((None, CONV_K, tn), lambda j, i: (layer, 0, j))],
        out_specs=pl.BlockSpec((tm, tn), lambda j, i: (i, j)),
        out_shape=jax.ShapeDtypeStruct((m, CONV_WIDTH), _BF16),
        scratch_shapes=[pltpu.VMEM((SUBLANES, tn), _F32)],
        compiler_params=_params("parallel", "arbitrary"),
        name="conv_mixer",
    )(xg, ssq, w_in, w_in, w_in, conv_w)


def _gate_low_kernel(xg_ref, ssq_ref, wa_ref, o_ref):
    o_ref[...] = _dot(xg_ref[...], wa_ref[...]) * _row_scale(ssq_ref, xg_ref.shape[1])


def _gate_low(xg, ssq, w_a, layer, tm=1024):
    m, d = xg.shape
    return pl.pallas_call(
        _gate_low_kernel,
        grid=(m // tm,),
        in_specs=[pl.BlockSpec((tm, d), lambda i: (i, 0)),
                  pl.BlockSpec((tm, LANES), lambda i: (i, 0)),
                  pl.BlockSpec((None, d, LANES), lambda i: (layer, 0, 0))],
        out_specs=pl.BlockSpec((tm, LANES), lambda i: (i, 0)),
        out_shape=jax.ShapeDtypeStruct((m, LANES), _F32),
        compiler_params=_params("parallel"),
        name="gate_low",
    )(xg, ssq, w_a)


def _gla_project_pieces(xg_ref, ssq_ref, w_refs, p_ref):
    d = xg_ref.shape[1]
    pieces = []
    for half in range(xg_ref.shape[0] // GLA_SUB):
        r = slice(half * GLA_SUB, (half + 1) * GLA_SUB)
        col = 0
        for w_ref in w_refs:
            width = w_ref.shape[1]

            def piece(r=r, col=col, width=width, w_ref=w_ref):
                rs = _row_scale(ssq_ref[r, :], d)
                p_ref[r, col:col + width] = _dot(xg_ref[r, :], w_ref[...]) * rs
            pieces.append(piece)
            col += width
    return pieces


def _gla_math(p_ref, a_ref, aup_ref, ab_ref, ng_ref, o_ref, st_ref, first_tile):
    rows = p_ref.shape[0]
    nchunk = GLA_SUB // CHUNK
    ri = lax.broadcasted_iota(jnp.int32, (GLA_SUB, GLA_SUB), 0)
    ci = lax.broadcasted_iota(jnp.int32, (GLA_SUB, GLA_SUB), 1)
    lower = (((ri // CHUNK) == (ci // CHUNK)) & (ci <= ri)).astype(_BF16)
    c_k, c_v, c_g = GLA_DK, 2 * GLA_DK, 2 * GLA_DK + GLA_DV
    chunk = lambda c: slice(c * CHUNK, (c + 1) * CHUNK)

    a_hi, a_lo = _split2(a_ref[...])
    w_hi, w_lo = _split2(aup_ref[...])
    z = ab_ref[...] + _dot(a_hi, w_hi) + _dot(a_lo, w_hi) + _dot(a_hi, w_lo)
    log_a = (jnp.minimum(z, 0.0) - jnp.log1p(jnp.exp(-jnp.abs(z)))) * (1.0 / GLA_GATE_TAU)
    la_hi, la_lo = _split2(log_a)
    yield

    state = jnp.where(first_tile, 0.0, st_ref[...])
    for s in range(rows // GLA_SUB):
        r = slice(s * GLA_SUB, (s + 1) * GLA_SUB)
        cum = _dot(lower, la_hi[r]) + _dot(lower, la_lo[r])
        totals = [cum[(c + 1) * CHUNK - 1:(c + 1) * CHUNK, :] for c in range(nchunk)]
        before = [jnp.zeros_like(totals[0])]
        for c in range(nchunk):
            before.append(before[c] + totals[c])
        total_b = jnp.concatenate([jnp.broadcast_to(tr, (CHUNK, GLA_DK)) for tr in totals], axis=0)
        k = p_ref[r, c_k:c_v]
        q_dec = p_ref[r, 0:c_k] * (GLA_DK ** -0.5) * jnp.exp(cum)
        k_inv = (k * jnp.exp(-cum)).astype(_BF16)
        k_dec = k * jnp.exp(total_b - cum)
        vb = p_ref[r, c_v:c_g].astype(_BF16)
        yield

        score_rows = []
        for c in range(nchunk):
            keys = [(k_dec[chunk(c2)] * jnp.exp(before[c] - before[c2 + 1])).astype(_BF16)
                    for c2 in range(c)]
            keys.append(k_inv[chunk(c)])
            if c + 1 < nchunk:
                keys.append(jnp.zeros(((nchunk - 1 - c) * CHUNK, GLA_DK), _BF16))
            score_rows.append(_dot_nt(q_dec[chunk(c)].astype(_BF16), jnp.concatenate(keys, axis=0)))
        scores = jnp.where(ci <= ri, jnp.concatenate(score_rows, axis=0), 0.0).astype(_BF16)
        yield

        q_state = jnp.concatenate([q_dec[chunk(c)] * jnp.exp(before[c]) for c in range(nchunk)], axis=0)
        o = _dot(scores, vb) + _dot_nt(q_state.astype(_BF16), state.astype(_BF16))
        k_end = jnp.concatenate([k_dec[chunk(c)] * jnp.exp(before[nchunk] - before[c + 1])
                                 for c in range(nchunk)], axis=0)
        yield
        state = state * jnp.exp(before[nchunk]) + _dot_tn(vb, k_end.astype(_BF16))
        yield

        o = o * lax.rsqrt(jnp.mean(o * o, axis=-1, keepdims=True) + NORM_EPS)
        g = p_ref[r, c_g:]
        o_ref[r, :] = (o * ng_ref[...] * (g / (1.0 + jnp.exp(-g)))).astype(o_ref.dtype)
        yield

    st_ref[...] = state


def _gla_kernel(xg_ref, ssq_ref, a_ref, wq_ref, wk_ref, wv_ref, wg_ref, aup_ref, ab_ref, ng_ref,
                o_ref, pa_ref, pb_ref, st_ref, *, tiles_per_seq):
    n = pl.program_id(1)
    w_refs = (wq_ref, wk_ref, wv_ref, wg_ref)

    @pl.when(n == 0)
    def _():
        pa_ref[...] = jnp.zeros_like(pa_ref)
        pb_ref[...] = jnp.zeros_like(pb_ref)
        st_ref[...] = jnp.zeros_like(st_ref)

    first_tile = (jnp.maximum(n - 1, 0) % tiles_per_seq) == 0

    def step(p_new, p_old):
        pieces = _gla_project_pieces(xg_ref, ssq_ref, w_refs, p_new)
        for _ in _gla_math(p_old, a_ref, aup_ref, ab_ref, ng_ref, o_ref, st_ref, first_tile):
            if pieces:
                pieces.pop(0)()
        for piece in pieces:
            piece()

    @pl.when(n % 2 == 0)
    def _():
        step(pa_ref, pb_ref)

    @pl.when(n % 2 == 1)
    def _():
        step(pb_ref, pa_ref)


def _gla_mixer(xg, ssq, a_low, w_in, a_up, a_bias, norm_g, layer, seq, tr=512):
    m, d = xg.shape
    nt = m // tr
    kern = functools.partial(_gla_kernel, tiles_per_seq=seq // tr)
    cur = lambda h, n: (jnp.minimum(n, nt - 1), 0)
    prev = lambda h, n: (jnp.maximum(n - 1, 0), 0)
    wspec = lambda off, width: pl.BlockSpec((None, d, width), lambda h, n: (layer, 0, off // width + h))
    return pl.pallas_call(
        kern,
        grid=(GLA_HEADS, nt + 1),
        in_specs=[pl.BlockSpec((tr, d), cur),
                  pl.BlockSpec((tr, LANES), cur),
                  pl.BlockSpec((tr, LANES), prev),
                  wspec(OFF_Q, GLA_DK), wspec(OFF_K, GLA_DK), wspec(OFF_V, GLA_DV), wspec(OFF_G, GLA_DV),
                  pl.BlockSpec((None, LANES, GLA_DK), lambda h, n: (layer, 0, h)),
                  pl.BlockSpec((None, 1, GLA_DK), lambda h, n: (layer, 0, h)),
                  pl.BlockSpec((None, 1, GLA_DV), lambda h, n: (layer, 0, h))],
        out_specs=pl.BlockSpec((tr, GLA_DV), lambda h, n: (jnp.maximum(n - 1, 0), h)),
        out_shape=jax.ShapeDtypeStruct((m, GLA_WIDTH), _BF16),
        scratch_shapes=[pltpu.VMEM((tr, GLA_COLS), _F32),
                        pltpu.VMEM((tr, GLA_COLS), _F32),
                        pltpu.VMEM((GLA_DV, GLA_DK), _F32)],
        compiler_params=_params("parallel", "arbitrary"),
        name="gla_mixer",
    )(xg, ssq, a_low, w_in, w_in, w_in, w_in, a_up, a_bias, norm_g)


def _init_ssq(ssq_ref):
    @pl.when(pl.program_id(1) == 0)
    def _():
        ssq_ref[...] = jnp.zeros_like(ssq_ref)


def _emit_stream(xn, gn_ref, o_ref, xg_ref, ssq_ref):
    o_ref[...] = xn
    xg_ref[...] = (xn * gn_ref[...]).astype(xg_ref.dtype)
    ssq_ref[...] += _lane_tile_sum(xn * xn)


def _out_kernel(yc_hbm, yg_hbm, w1_ref, w2_ref, x_ref, gn_ref, o_ref, xg_ref, ssq_ref,
                ycbuf, ygbuf, sem_c, sem_g):
    slot = _fetch_row_tile(yc_hbm, ycbuf, sem_c)
    _fetch_row_tile(yg_hbm, ygbuf, sem_g)
    _init_ssq(ssq_ref)
    xn = x_ref[...] + _dot(ycbuf[slot], w1_ref[...]) + _dot(ygbuf[slot], w2_ref[...])
    _emit_stream(xn, gn_ref, o_ref, xg_ref, ssq_ref)


def _stream_out(m, d, tm, tn):
    specs = [pl.BlockSpec((tm, tn), lambda i, j: (i, j)),
             pl.BlockSpec((tm, tn), lambda i, j: (i, j)),
             pl.BlockSpec((tm, LANES), lambda i, j: (i, 0))]
    shapes = [jax.ShapeDtypeStruct((m, d), _F32),
              jax.ShapeDtypeStruct((m, d), _BF16),
              jax.ShapeDtypeStruct((m, LANES), _F32)]
    return specs, shapes


def _out_proj(y_conv, y_gla, w_out, x, g_next, layer, tm=1024, tn=512):
    m, d = x.shape
    kc = y_conv.shape[1]
    kg = y_gla.shape[1]
    out_specs, out_shape = _stream_out(m, d, tm, tn)
    return pl.pallas_call(
        _out_kernel,
        grid=(m // tm, d // tn),
        in_specs=[pl.BlockSpec(memory_space=pl.ANY),
                  pl.BlockSpec(memory_space=pl.ANY),
                  pl.BlockSpec((None, kc, tn), lambda i, j: (layer, 0, j)),
                  pl.BlockSpec((None, kg, tn), lambda i, j: (layer, 1, j)),
                  pl.BlockSpec((tm, tn), lambda i, j: (i, j)),
                  pl.BlockSpec((1, tn), lambda i, j: (0, j))],
        out_specs=out_specs,
        out_shape=out_shape,
        scratch_shapes=[pltpu.VMEM((2, tm, kc), y_conv.dtype), pltpu.VMEM((2, tm, kg), y_gla.dtype),
                        pltpu.SemaphoreType.DMA((2,)), pltpu.SemaphoreType.DMA((2,))],
        compiler_params=_params("arbitrary", "arbitrary"),
        name="out_proj",
    )(y_conv, y_gla, w_out, w_out, x, g_next.reshape(1, d))


def _fetch_row_tile(h_hbm, hbuf, sem):
    i, j = pl.program_id(0), pl.program_id(1)
    tm = hbuf.shape[1]
    slot = i % 2

    def tile_copy(tile, s):
        rows = pl.ds(pl.multiple_of(tile * tm, tm), tm)
        return pltpu.make_async_copy(h_hbm.at[rows, :], hbuf.at[s], sem.at[s])

    @pl.when(j == 0)
    def _():
        @pl.when(i == 0)
        def _():
            tile_copy(0, 0).start()

        tile_copy(i, slot).wait()

        @pl.when(i + 1 < pl.num_programs(0))
        def _():
            tile_copy(i + 1, 1 - slot).start(priority=1)

    return slot


def _down_kernel(h_hbm, w_ref, x_ref, gn_ref, o_ref, xg_ref, ssq_ref, hbuf, sem):
    slot = _fetch_row_tile(h_hbm, hbuf, sem)
    _init_ssq(ssq_ref)
    xn = x_ref[...] + _dot(hbuf[slot], w_ref[...])
    _emit_stream(xn, gn_ref, o_ref, xg_ref, ssq_ref)


def _down_last_kernel(h_hbm, w_ref, x_ref, o_ref, hbuf, sem):
    slot = _fetch_row_tile(h_hbm, hbuf, sem)
    o_ref[...] = x_ref[...] + _dot(hbuf[slot], w_ref[...])


def _down_proj(h, w_down, x, g_next, layer, tm=512, tn=512):
    m, d = x.shape
    f = h.shape[1]
    in_specs = [pl.BlockSpec(memory_space=pl.ANY),
                pl.BlockSpec((None, f, tn), lambda i, j: (layer, 0, j)),
                pl.BlockSpec((tm, tn), lambda i, j: (i, j))]
    scratch = [pltpu.VMEM((2, tm, f), h.dtype), pltpu.SemaphoreType.DMA((2,))]
    params = _params("arbitrary", "arbitrary")
    if g_next is None:
        return pl.pallas_call(
            _down_last_kernel,
            grid=(m // tm, d // tn),
            in_specs=in_specs,
            out_specs=pl.BlockSpec((tm, tn), lambda i, j: (i, j)),
            out_shape=jax.ShapeDtypeStruct((m, d), _F32),
            scratch_shapes=scratch,
            compiler_params=params,
            name="ffn_down_last",
        )(h, w_down, x)
    out_specs, out_shape = _stream_out(m, d, tm, tn)
    return pl.pallas_call(
        _down_kernel,
        grid=(m // tm, d // tn),
        in_specs=in_specs + [pl.BlockSpec((1, tn), lambda i, j: (0, j))],
        out_specs=out_specs,
        out_shape=out_shape,
        scratch_shapes=scratch,
        compiler_params=params,
        name="ffn_down",
    )(h, w_down, x, g_next.reshape(1, d))


def _gate_up_kernel(xg_hbm, ssq_ref, wg_ref, wu_ref, o_ref, xbuf, sem, *, sub):
    slot = _fetch_row_tile(xg_hbm, xbuf, sem)
    wg = wg_ref[...].astype(_BF16)
    wu = wu_ref[...].astype(_BF16)
    rs_all = _row_scale(ssq_ref, xbuf.shape[2])
    for r in range(xbuf.shape[1] // sub):
        sl = slice(r * sub, (r + 1) * sub)
        xg = xbuf[slot, sl, :]
        rs = rs_all[sl]
        g = _dot(xg, wg) * rs
        u = _dot(xg, wu) * rs
        o_ref[sl, :] = ((g / (1.0 + jnp.exp(-g))) * u).astype(o_ref.dtype)


def _gate_up(xg, ssq, w_gate, w_up, layer, tm=2048, tf=256, sub=512):
    m, d = xg.shape
    f = w_gate.shape[2]
    return pl.pallas_call(
        functools.partial(_gate_up_kernel, sub=sub),
        grid=(m // tm, f // tf),
        in_specs=[pl.BlockSpec(memory_space=pl.ANY),
                  pl.BlockSpec((tm, LANES), lambda i, j: (i, 0)),
                  pl.BlockSpec((None, d, tf), lambda i, j: (layer, 0, j)),
                  pl.BlockSpec((None, d, tf), lambda i, j: (layer, 0, j))],
        out_specs=pl.BlockSpec((tm, tf), lambda i, j: (i, j)),
        out_shape=jax.ShapeDtypeStruct((m, f), _BF16),
        scratch_shapes=[pltpu.VMEM((2, tm, d), xg.dtype), pltpu.SemaphoreType.DMA((2,))],
        compiler_params=_params("arbitrary", "arbitrary", vmem=VMEM_LIMIT_GATE_UP),
        name="ffn_gate_up",
    )(xg, ssq, w_gate, w_up)


def kernel(x, mix_norm, w_in, conv_w, gla_a_up, gla_a_bias, gla_norm, w_out,
           ffn_norm, w_gate, w_up, w_down, final_norm):
    batch, seq, d = x.shape
    depth = w_in.shape[0]
    xf = x.reshape(batch * seq, d)

    w_in_b = w_in.astype(_BF16)
    w_a = jnp.pad(w_in[:, :, OFF_A:], ((0, 0), (0, 0), (0, LANES - GLA_GATE_RANK))).astype(_BF16)
    w_out_b = w_out.astype(_BF16)
    w_down_b = w_down.astype(_BF16)
    a_up = jnp.pad(gla_a_up, ((0, 0), (0, LANES - GLA_GATE_RANK), (0, 0)))
    a_bias = gla_a_bias.reshape(depth, 1, -1)
    norm_g = gla_norm.reshape(depth, 1, -1)

    xg, ssq = _prep(xf, mix_norm[0])
    for l in range(depth):
        y_conv = _conv_mixer(xg, ssq, w_in_b, conv_w, l, seq)
        a_low = _gate_low(xg, ssq, w_a, l)
        y_gla = _gla_mixer(xg, ssq, a_low, w_in_b, a_up, a_bias, norm_g, l, seq)
        xf, xg, ssq = _out_proj(y_conv, y_gla, w_out_b, xf, ffn_norm[l], l)
        h = _gate_up(xg, ssq, w_gate, w_up, l)
        if l + 1 < depth:
            xf, xg, ssq = _down_proj(h, w_down_b, xf, mix_norm[l + 1], l)
        else:
            xf = _down_proj(h, w_down_b, xf, None, l)

    return _rmsnorm(xf, final_norm, x.dtype).reshape(batch, seq, d)
```
